```python
import math
import jax
import jax.numpy as jnp
from jax import lax
import numpy as np

D_MODEL = 1024
BATCH = 4
SEQ = 8192
DEPTH = 2

GRID_W = 64
CTX_LEN = 256
ROPE_BASE = 10000.0
EPS = 1e-6
N_MOD = 6
N_EVEN = (DEPTH + 1) // 2
N_ODD = DEPTH // 2
ATTN_BLK = 128
CONV_W = 4
CONV_PAD = (2, 1)

MLA_HEADS = 8
MLA_NOPE = 64
MLA_ROPE = 32
MLA_QK = MLA_NOPE + MLA_ROPE
MLA_V = 64
MLA_Q_RANK = 384
MLA_KV_RANK = 256
LRU_WIDTH = 512
LRU_BLOCKS = 8
LRU_BLOCK_DIM = LRU_WIDTH // LRU_BLOCKS
LRU_C = 8.0
AB_IN = MLA_Q_RANK + MLA_KV_RANK + MLA_ROPE + 2 * LRU_WIDTH
AB_OUT = MLA_HEADS * MLA_V + LRU_WIDTH
SWA_HEADS = 8
SWA_KV_HEADS = 2
SWA_HD = 64
WINDOW = 128
SSD_HEADS = 8
SSD_HD = 64
SSD_INNER = SSD_HEADS * SSD_HD
SSD_STATE = 128
SSD_GROUPS = 2
SSD_CHUNK = 128
SSD_CONV_DIM = SSD_INNER + 2 * SSD_GROUPS * SSD_STATE
CD_IN = SWA_HEADS * SWA_HD + 2 * SWA_KV_HEADS * SWA_HD + SSD_INNER + SSD_CONV_DIM + 2 * SSD_HEADS
CD_OUT = SWA_HEADS * SWA_HD + SSD_INNER
N_EXPERTS = 32
N_GROUPS = 4
EXPERTS_PER_GROUP = N_EXPERTS // N_GROUPS
TOP_K = 2
D_EXPERT = 512
MOE_BLK = 256

F32 = jnp.float32

kernel_name = 'hybrid_mla_rglru_swa_ssd_moe_dit'


def rmsnorm(x, g):
    xf = x.astype(F32)
    y = xf * lax.rsqrt(jnp.mean(xf * xf, axis=-1, keepdims=True) + EPS)
    return (y * g.astype(F32)).astype(x.dtype)


def modulate(x, g, shift, scale):
    return rmsnorm(x, g) * (1.0 + scale) + shift


def grid_positions(n):
    rows = n // GRID_W
    row = jnp.repeat(jnp.arange(rows, dtype=F32), GRID_W)
    col = jnp.tile(jnp.arange(GRID_W, dtype=F32), rows)
    return row, col


def rope_1d(x, pos):
    f = x.shape[-1] // 2
    inv = ROPE_BASE ** (-jnp.arange(f, dtype=F32) / f)
    ang = pos[:, None] * inv[None, :]
    cos = jnp.cos(ang)[None, :, None, :]
    sin = jnp.sin(ang)[None, :, None, :]
    x1 = x[..., :f].astype(F32)
    x2 = x[..., f:].astype(F32)
    return jnp.concatenate([x1 * cos - x2 * sin, x1 * sin + x2 * cos], axis=-1).astype(x.dtype)


def rope_2d(x, row, col):
    half = x.shape[-1] // 2
    return jnp.concatenate([rope_1d(x[..., :half], row), rope_1d(x[..., half:], col)], axis=-1)


def centred_dwconv(x, w, b):
    y = lax.conv_general_dilated(x, w[:, None, :].astype(x.dtype), window_strides=(1,), padding=[CONV_PAD], dimension_numbers=('NWC', 'WIO', 'NWC'), feature_group_count=x.shape[-1])
    return y + b.astype(x.dtype)


def full_attention(q, k, v):
    sc = jnp.einsum('bqhd,bkhd->bhqk', q, k).astype(F32) * q.shape[-1] ** -0.5
    p = jax.nn.softmax(sc, axis=-1).astype(v.dtype)
    return jnp.einsum('bhqk,bkhd->bqhd', p, v)


def dense_attention_blocks(q, k, v):
    b, s, h, dq = q.shape
    nb = s // ATTN_BLK
    qb = jnp.moveaxis(q.reshape(b, nb, ATTN_BLK, h, dq), 1, 0)
    out = lax.map(lambda qi: full_attention(qi, k, v), qb)
    return jnp.moveaxis(out, 0, 1).reshape(b, s, h * v.shape[-1])


def sink_logits(sink, hkv, grp, lead_shape):
    return jnp.broadcast_to(sink.reshape(hkv, grp)[None, :, :, None, None].astype(F32), lead_shape + (1,))


def sink_attention(q, k, v, sink):
    b, n, hq, hd = q.shape
    hkv = k.shape[2]
    grp = hq // hkv
    qg = q.reshape(b, n, hkv, grp, hd)
    sc = jnp.einsum('bqkgd,bckd->bkgqc', qg, k).astype(F32) * hd ** -0.5
    sk = sink_logits(sink, hkv, grp, sc.shape[:-1])
    p = jax.nn.softmax(jnp.concatenate([sk, sc], axis=-1), axis=-1)[..., 1:].astype(v.dtype)
    return jnp.einsum('bkgqc,bckd->bqkgd', p, v).reshape(b, n, hq * hd)


def window_attention(q, k, v, k_ctx, v_ctx, sink):
    b, s, hq, hd = q.shape
    hkv = k.shape[2]
    grp = hq // hkv
    nb = s // ATTN_BLK
    n_ctx = k_ctx.shape[1]
    scale = hd ** -0.5
    qb = jnp.moveaxis(q.reshape(b, nb, ATTN_BLK, hkv, grp, hd), 1, 0)

    def band(t):
        tb = jnp.pad(t.reshape(b, nb, ATTN_BLK, hkv, hd), ((0, 0), (1, 1), (0, 0), (0, 0), (0, 0)))
        return jnp.moveaxis(jnp.concatenate([tb[:, :-2], tb[:, 1:-1], tb[:, 2:]], axis=2), 1, 0)

    kb, vb = band(k), band(v)
    q_off = jnp.arange(ATTN_BLK)[:, None]
    k_off = jnp.arange(3 * ATTN_BLK)[None, :] - ATTN_BLK
    near = jnp.abs(k_off - q_off) <= WINDOW
    sk = sink_logits(sink, hkv, grp, (b, hkv, grp, ATTN_BLK))

    def one(args):
        blk, qi, ki, vi = args
        kpos = blk * ATTN_BLK + k_off
        valid = near & (kpos >= 0) & (kpos < s)
        s_band = jnp.where(valid, jnp.einsum('bqkgd,bskd->bkgqs', qi, ki).astype(F32) * scale, -jnp.inf)
        s_ctx = jnp.einsum('bqkgd,bckd->bkgqc', qi, k_ctx).astype(F32) * scale
        p = jax.nn.softmax(jnp.concatenate([sk, s_ctx, s_band], axis=-1), axis=-1).astype(v.dtype)
        return (jnp.einsum('bkgqc,bckd->bqkgd', p[..., 1:1 + n_ctx], v_ctx)
                + jnp.einsum('bkgqs,bskd->bqkgd', p[..., 1 + n_ctx:], vi))

    out = lax.map(one, (jnp.arange(nb), qb, kb, vb))
    return jnp.moveaxis(out, 0, 1).reshape(b, s, hq * hd)


def block_diag(x, w):
    b, n, _ = x.shape
    nblk, di, do = w.shape
    return jnp.einsum('blhi,hij->blhj', x.reshape(b, n, nblk, di), w).reshape(b, n, nblk * do)


def linear_scan(a, b, h0, reverse):
    def comb(l, r):
        return r[0] * l[0], r[0] * l[1] + r[1]
    a_cum, b_cum = lax.associative_scan(comb, (a, b), reverse=reverse, axis=1)
    return a_cum * h0[:, None, :] + b_cum


def rglru_direction(xc, w_a, b_a, w_x, b_x, lam, h0, reverse):
    r = jax.nn.sigmoid(block_diag(xc, w_a) + b_a)
    i = jax.nn.sigmoid(block_diag(xc, w_x) + b_x)
    log_a = -LRU_C * r * jax.nn.softplus(-lam.astype(F32))
    bterm = jnp.sqrt(-jnp.expm1(2.0 * log_a)) * (i * xc)
    return linear_scan(jnp.exp(log_a), bterm, h0, reverse)


def segsum(a):
    t = a.shape[-1]
    ar = jnp.broadcast_to(a[..., :, None], a.shape + (t,))
    ar = jnp.where(jnp.tril(jnp.ones((t, t), bool), -1), ar, 0.0)
    cs = jnp.cumsum(ar, axis=-2)
    return jnp.where(jnp.tril(jnp.ones((t, t), bool)), cs, -jnp.inf)


def ssd_chunked(x, dt, A, bm, cm, h0, want_y):
    b, n, h, p = x.shape
    nc = n // SSD_CHUNK
    rep = h // bm.shape[2]
    bh = jnp.repeat(bm, rep, axis=2).reshape(b, nc, SSD_CHUNK, h, -1)
    ch = jnp.repeat(cm, rep, axis=2).reshape(b, nc, SSD_CHUNK, h, -1)
    xdt = (x * dt[..., None]).reshape(b, nc, SSD_CHUNK, h, p)
    a = jnp.moveaxis((dt * A).reshape(b, nc, SSD_CHUNK, h), -1, 1)
    a_cum = jnp.cumsum(a, axis=-1)
    decay_to_end = jnp.exp(a_cum[..., -1:] - a_cum)
    states = jnp.einsum('bclhn,bhcl,bclhp->bchpn', bh, decay_to_end, xdt)
    states = jnp.concatenate([h0[:, None], states], axis=1)
    chunk_decay = jnp.exp(segsum(jnp.pad(a_cum[..., -1], ((0, 0), (0, 0), (1, 0)))))
    states = jnp.einsum('bhzc,bchpn->bzhpn', chunk_decay, states)
    final = states[:, -1]
    if not want_y:
        return None, final
    scores = jnp.einsum('bclhn,bcshn->bhcls', ch, bh) * jnp.exp(segsum(a))
    y_diag = jnp.einsum('bhcls,bcshp->bclhp', scores, xdt)
    y_off = jnp.einsum('bclhn,bchpn,bhcl->bclhp', ch, states[:, :-1], jnp.exp(a_cum))
    return (y_diag + y_off).reshape(b, n, h, p), final


def ssd_direction(xs, dt_raw, bm, cm, a_log, dt_bias, d_skip, h0, reverse, want_y):
    A = -jnp.exp(a_log.astype(F32))
    dt = jax.nn.softplus(dt_raw + dt_bias.astype(F32))
    flip = (lambda t: jnp.flip(t, axis=1)) if reverse else (lambda t: t)
    y, h_fin = ssd_chunked(flip(xs), flip(dt), A, flip(bm), flip(cm), h0, want_y)
    if want_y:
        y = flip(y) + d_skip.astype(F32)[:, None] * xs
    return y, h_fin


def mixer_ab(h_lat, h_ctx, row, col, need_ctx, w_in, w_out, q_norm_g, w_q_up, kv_norm_g, w_kv_up,
             q_head_g, k_head_g, conv_w, conv_b, w_a, b_a, w_x, b_x, lam):
    cuts = np.cumsum([MLA_Q_RANK, MLA_KV_RANK, MLA_ROPE, LRU_WIDTH]).tolist()
    cq_l, ckv_l, kr_l, xr_l, gr_l = jnp.split(h_lat @ w_in, cuts, axis=-1)
    cq_c, ckv_c, kr_c, xr_c, gr_c = jnp.split(h_ctx @ w_in, cuts, axis=-1)
    bsz, n_ctx = h_ctx.shape[:2]

    def heads_q(cq):
        b, n = cq.shape[:2]
        q = (rmsnorm(cq, q_norm_g) @ w_q_up).reshape(b, n, MLA_HEADS, MLA_QK)
        return rmsnorm(q, q_head_g)

    def heads_kv(ckv, kr):
        b, n = ckv.shape[:2]
        kv = (rmsnorm(ckv, kv_norm_g) @ w_kv_up).reshape(b, n, MLA_HEADS, MLA_NOPE + MLA_V)
        k_rope = jnp.broadcast_to(kr[:, :, None, :], (b, n, MLA_HEADS, MLA_ROPE))
        k = rmsnorm(jnp.concatenate([kv[..., :MLA_NOPE], k_rope], axis=-1), k_head_g)
        return k, kv[..., MLA_NOPE:]

    def rope_tail(t):
        return jnp.concatenate([t[..., :MLA_NOPE], rope_2d(t[..., MLA_NOPE:], row, col)], axis=-1)

    k_c, v_c = heads_kv(ckv_c, kr_c)
    k_l, v_l = heads_kv(ckv_l, kr_l)
    q_l = rope_tail(heads_q(cq_l))
    k_l = rope_tail(k_l)
    attn_l = dense_attention_blocks(q_l, jnp.concatenate([k_c, k_l], axis=1), jnp.concatenate([v_c, v_l], axis=1))

    xc_l = centred_dwconv(xr_l, conv_w, conv_b).astype(F32)
    xc_c = centred_dwconv(xr_c, conv_w, conv_b).astype(F32)
    zero = jnp.zeros((bsz, LRU_WIDTH), F32)
    hs_l, hs_c = [], []
    for d, rev in enumerate((False, True)):
        prm = (w_a[d], b_a[d], w_x[d], b_x[d], lam[d])
        h_c = rglru_direction(xc_c, *prm, zero, rev)
        h_end = h_c[:, 0] if rev else h_c[:, -1]
        hs_l.append(rglru_direction(xc_l, *prm, h_end, rev))
        hs_c.append(h_c)
    lru_l = (hs_l[0] + hs_l[1]).astype(h_lat.dtype) * jax.nn.gelu(gr_l)
    y_l = jnp.concatenate([attn_l, lru_l], axis=-1) @ w_out
    if not need_ctx:
        return y_l, None
    attn_c = full_attention(heads_q(cq_c), k_c, v_c).reshape(bsz, n_ctx, -1)
    lru_c = (hs_c[0] + hs_c[1]).astype(h_ctx.dtype) * jax.nn.gelu(gr_c)
    y_c = jnp.concatenate([attn_c, lru_c], axis=-1) @ w_out
    return y_l, y_c


def mixer_cd(h_lat, h_ctx, row, col, need_ctx, w_in, w_out, q_head_g, k_head_g, sink,
             conv_w, conv_b, a_log, dt_bias, d_skip, norm_g):
    q_w = SWA_HEADS * SWA_HD
    kv_w = SWA_KV_HEADS * SWA_HD
    cuts = np.cumsum([q_w, kv_w, kv_w, SSD_INNER, SSD_CONV_DIM]).tolist()
    q_l, k_l, v_l, z_l, xbc_l, dt_l = jnp.split(h_lat @ w_in, cuts, axis=-1)
    q_c, k_c, v_c, z_c, xbc_c, dt_c = jnp.split(h_ctx @ w_in, cuts, axis=-1)
    bsz, n_lat = h_lat.shape[:2]
    n_ctx = h_ctx.shape[1]

    def heads(t, nh):
        return t.reshape(t.shape[0], t.shape[1], nh, SWA_HD)

    k_c = rmsnorm(heads(k_c, SWA_KV_HEADS), k_head_g)
    v_c = heads(v_c, SWA_KV_HEADS)
    q_lh = rope_2d(rmsnorm(heads(q_l, SWA_HEADS), q_head_g), row, col)
    k_lh = rope_2d(rmsnorm(heads(k_l, SWA_KV_HEADS), k_head_g), row, col)
    attn_l = window_attention(q_lh, k_lh, heads(v_l, SWA_KV_HEADS), k_c, v_c, sink)

    def ssd_prep(xbc, dtr):
        b, n = xbc.shape[:2]
        u = jax.nn.silu(centred_dwconv(xbc, conv_w, conv_b)).astype(F32)
        xs = u[..., :SSD_INNER].reshape(b, n, SSD_HEADS, SSD_HD)
        bm = u[..., SSD_INNER:SSD_INNER + SSD_GROUPS * SSD_STATE].reshape(b, n, SSD_GROUPS, SSD_STATE)
        cm = u[..., SSD_INNER + SSD_GROUPS * SSD_STATE:].reshape(b, n, SSD_GROUPS, SSD_STATE)
        return xs, bm, cm, dtr.astype(F32).reshape(b, n, 2, SSD_HEADS)

    xs_l, bm_l, cm_l, dtr_l = ssd_prep(xbc_l, dt_l)
    xs_c, bm_c, cm_c, dtr_c = ssd_prep(xbc_c, dt_c)
    h0 = jnp.zeros((bsz, SSD_HEADS, SSD_HD, SSD_STATE), F32)
    ys_l, ys_c = [], []
    for d in range(2):
        y_c, h_c = ssd_direction(xs_c, dtr_c[:, :, d], bm_c, cm_c, a_log[d], dt_bias[d], d_skip[d], h0, d == 1, need_ctx)
        y_l, _ = ssd_direction(xs_l, dtr_l[:, :, d], bm_l, cm_l, a_log[d], dt_bias[d], d_skip[d], h_c, d == 1, True)
        ys_l.append(y_l)
        ys_c.append(y_c)

    def gated_out(ys, z, n):
        y = (ys[0] + ys[1]).reshape(bsz, n, SSD_INNER).astype(z.dtype)
        return rmsnorm(y * jax.nn.silu(z), norm_g)

    y_lat = jnp.concatenate([attn_l, gated_out(ys_l, z_l, n_lat)], axis=-1) @ w_out
    if not need_ctx:
        return y_lat, None
    q_ch = rmsnorm(heads(q_c, SWA_HEADS), q_head_g)
    attn_c = sink_attention(q_ch, k_c, v_c, sink)
    y_ctx = jnp.concatenate([attn_c, gated_out(ys_c, z_c, n_ctx)], axis=-1) @ w_out
    return y_lat, y_ctx


def moe_ffn(h, router_w, router_b, w_gate, w_up, w_down):
    n, d = h.shape
    scores = jax.nn.sigmoid((h @ router_w).astype(F32))
    biased = (scores + router_b.astype(F32)).reshape(n, N_GROUPS, EXPERTS_PER_GROUP)
    group_score = lax.top_k(biased, TOP_K)[0].sum(-1)
    g_sel = jnp.argmax(group_score, axis=-1)
    in_group = jnp.take_along_axis(biased, g_sel[:, None, None], axis=1)[:, 0]
    _, local = lax.top_k(in_group, TOP_K)
    e_sel = g_sel[:, None] * EXPERTS_PER_GROUP + local
    s_sel = jnp.take_along_axis(scores, e_sel, axis=1)
    w_sel = s_sel / jnp.sum(s_sel, axis=-1, keepdims=True)

    m = n * TOP_K
    n_blocks = -(-m // MOE_BLK) + N_EXPERTS
    e_flat = e_sel.reshape(m)
    tok_flat = jnp.repeat(jnp.arange(n, dtype=jnp.int32), TOP_K)
    w_flat = w_sel.reshape(m)
    counts = jnp.bincount(e_flat, length=N_EXPERTS)
    padded = (counts + MOE_BLK - 1) // MOE_BLK * MOE_BLK
    pad_end = jnp.cumsum(padded)
    pad_start = pad_end - padded
    start = jnp.cumsum(counts) - counts
    order = jnp.argsort(e_flat)
    e_sorted = e_flat[order]
    dest = pad_start[e_sorted] + jnp.arange(m, dtype=jnp.int32) - start[e_sorted]
    slot_tok = jnp.full((n_blocks * MOE_BLK,), n, jnp.int32).at[dest].set(tok_flat[order])
    slot_w = jnp.zeros((n_blocks * MOE_BLK,), h.dtype).at[dest].set(w_flat[order].astype(h.dtype))
    block_expert = jnp.minimum(jnp.searchsorted(pad_end, jnp.arange(n_blocks) * MOE_BLK, side='right'), N_EXPERTS - 1)
    h_pad = jnp.concatenate([h, jnp.zeros((1, d), h.dtype)], axis=0)
    xb = h_pad[slot_tok].reshape(n_blocks, MOE_BLK, d)

    def expert_block(args):
        xi, e = args
        return (jax.nn.silu(xi @ w_gate[e]) * (xi @ w_up[e])) @ w_down[e]

    yb = lax.map(expert_block, (xb, block_expert)).reshape(n_blocks * MOE_BLK, d)
    return jax.ops.segment_sum(yb * slot_w[:, None], slot_tok, num_segments=n + 1)[:n]


def setup_inputs(seed: int = 0) -> dict:
    key = jax.random.key(seed)
    keys = iter(jax.random.split(key, 64))

    def normal(shape, scale):
        return jax.random.normal(next(keys), shape, F32) * scale

    def gain(shape):
        return 1.0 + normal(shape, 0.02)

    def uniform(shape, lo, hi):
        return jax.random.uniform(next(keys), shape, F32, lo, hi)

    lru_a = uniform((N_EVEN, 2, LRU_WIDTH), 0.9, 0.999) ** (1.0 / LRU_C)
    ssd_dt = jnp.exp(uniform((N_ODD, 2, SSD_HEADS), math.log(1e-3), math.log(1e-1)))
    return {
        'x': normal((BATCH, SEQ, D_MODEL), 1.0),
        'c': normal((BATCH, D_MODEL), 1.0),
        'ctx': normal((BATCH, CTX_LEN, D_MODEL), 1.0),
        'c_ctx': normal((D_MODEL,), 1.0),
        'ada_w': normal((DEPTH, D_MODEL, N_MOD * D_MODEL), D_MODEL ** -0.5),
        'ada_b': normal((DEPTH, N_MOD * D_MODEL), 0.02),
        'norm1_g': gain((DEPTH, D_MODEL)),
        'norm2_g': gain((DEPTH, D_MODEL)),
        'ab_w_in': normal((N_EVEN, D_MODEL, AB_IN), D_MODEL ** -0.5),
        'ab_w_out': normal((N_EVEN, AB_OUT, D_MODEL), AB_OUT ** -0.5),
        'mla_q_norm_g': gain((N_EVEN, MLA_Q_RANK)),
        'mla_w_q_up': normal((N_EVEN, MLA_Q_RANK, MLA_HEADS * MLA_QK), MLA_Q_RANK ** -0.5),
        'mla_kv_norm_g': gain((N_EVEN, MLA_KV_RANK)),
        'mla_w_kv_up': normal((N_EVEN, MLA_KV_RANK, MLA_HEADS * (MLA_NOPE + MLA_V)), MLA_KV_RANK ** -0.5),
        'mla_q_head_g': gain((N_EVEN, MLA_QK)),
        'mla_k_head_g': gain((N_EVEN, MLA_QK)),
        'lru_conv_w': normal((N_EVEN, CONV_W, LRU_WIDTH), CONV_W ** -0.5),
        'lru_conv_b': normal((N_EVEN, LRU_WIDTH), 0.02),
        'lru_w_a': normal((N_EVEN, 2, LRU_BLOCKS, LRU_BLOCK_DIM, LRU_BLOCK_DIM), LRU_BLOCK_DIM ** -0.5),
        'lru_b_a': normal((N_EVEN, 2, LRU_WIDTH), 0.02),
        'lru_w_x': normal((N_EVEN, 2, LRU_BLOCKS, LRU_BLOCK_DIM, LRU_BLOCK_DIM), LRU_BLOCK_DIM ** -0.5),
        'lru_b_x': normal((N_EVEN, 2, LRU_WIDTH), 0.02),
        'lru_lambda': jnp.log(lru_a) - jnp.log1p(-lru_a),
        'cd_w_in': normal((N_ODD, D_MODEL, CD_IN), D_MODEL ** -0.5),
        'cd_w_out': normal((N_ODD, CD_OUT, D_MODEL), CD_OUT ** -0.5),
        'swa_q_head_g': gain((N_ODD, SWA_HD)),
        'swa_k_head_g': gain((N_ODD, SWA_HD)),
        'swa_sink': normal((N_ODD, SWA_HEADS), 0.5),
        'ssd_conv_w': normal((N_ODD, CONV_W, SSD_CONV_DIM), CONV_W ** -0.5),
        'ssd_conv_b': normal((N_ODD, SSD_CONV_DIM), 0.02),
        'ssd_a_log': jnp.log(uniform((N_ODD, 2, SSD_HEADS), 1.0, 16.0)),
        'ssd_dt_bias': ssd_dt + jnp.log(-jnp.expm1(-ssd_dt)),
        'ssd_d': gain((N_ODD, 2, SSD_HEADS)),
        'ssd_norm_g': gain((N_ODD, SSD_INNER)),
        'router_w': normal((D_MODEL, N_EXPERTS), D_MODEL ** -0.5),
        'router_b': normal((N_EXPERTS,), 0.01),
        'moe_w_gate': normal((DEPTH, N_EXPERTS, D_MODEL, D_EXPERT), D_MODEL ** -0.5),
        'moe_w_up': normal((DEPTH, N_EXPERTS, D_MODEL, D_EXPERT), D_MODEL ** -0.5),
        'moe_w_down': normal((DEPTH, N_EXPERTS, D_EXPERT, D_MODEL), D_EXPERT ** -0.5),
    }


def reference(x, c, ctx, c_ctx, ada_w, ada_b, norm1_g, norm2_g,
              ab_w_in, ab_w_out, mla_q_norm_g, mla_w_q_up, mla_kv_norm_g, mla_w_kv_up, mla_q_head_g, mla_k_head_g,
              lru_conv_w, lru_conv_b, lru_w_a, lru_b_a, lru_w_x, lru_b_x, lru_lambda,
              cd_w_in, cd_w_out, swa_q_head_g, swa_k_head_g, swa_sink,
              ssd_conv_w, ssd_conv_b, ssd_a_log, ssd_dt_bias, ssd_d, ssd_norm_g,
              router_w, router_b, moe_w_gate, moe_w_up, moe_w_down):
    bsz, n_lat, d = x.shape
    n_ctx = ctx.shape[1]
    row, col = grid_positions(n_lat)
    x_lat, x_ctx = x, ctx
    for layer in range(DEPTH):
        need_ctx = layer < DEPTH - 1
        mod_l = (jax.nn.silu(c) @ ada_w[layer] + ada_b[layer]).reshape(bsz, N_MOD, 1, d)
        mod_c = (jax.nn.silu(c_ctx) @ ada_w[layer] + ada_b[layer]).reshape(N_MOD, d)
        h_lat = modulate(x_lat, norm1_g[layer], mod_l[:, 0], mod_l[:, 1])
        h_ctx = modulate(x_ctx, norm1_g[layer], mod_c[0], mod_c[1])
        i = layer // 2
        if layer % 2 == 0:
            y_lat, y_ctx = mixer_ab(h_lat, h_ctx, row, col, need_ctx, ab_w_in[i], ab_w_out[i],
                                    mla_q_norm_g[i], mla_w_q_up[i], mla_kv_norm_g[i], mla_w_kv_up[i],
                                    mla_q_head_g[i], mla_k_head_g[i], lru_conv_w[i], lru_conv_b[i],
                                    lru_w_a[i], lru_b_a[i], lru_w_x[i], lru_b_x[i], lru_lambda[i])
        else:
            y_lat, y_ctx = mixer_cd(h_lat, h_ctx, row, col, need_ctx, cd_w_in[i], cd_w_out[i],
                                    swa_q_head_g[i], swa_k_head_g[i], swa_sink[i], ssd_conv_w[i], ssd_conv_b[i],
                                    ssd_a_log[i], ssd_dt_bias[i], ssd_d[i], ssd_norm_g[i])
        x_lat = x_lat + mod_l[:, 2] * y_lat
        h2_lat = modulate(x_lat, norm2_g[layer], mod_l[:, 3], mod_l[:, 4]).reshape(bsz * n_lat, d)
        if need_ctx:
            x_ctx = x_ctx + mod_c[2] * y_ctx
            h2_ctx = modulate(x_ctx, norm2_g[layer], mod_c[3], mod_c[4]).reshape(bsz * n_ctx, d)
            f = moe_ffn(jnp.concatenate([h2_lat, h2_ctx], axis=0), router_w, router_b,
                        moe_w_gate[layer], moe_w_up[layer], moe_w_down[layer])
            x_ctx = x_ctx + mod_c[5] * f[bsz * n_lat:].reshape(bsz, n_ctx, d)
            f_lat = f[:bsz * n_lat]
        else:
            f_lat = moe_ffn(h2_lat, router_w, router_b, moe_w_gate[layer], moe_w_up[layer], moe_w_down[layer])
        x_lat = x_lat + mod_l[:, 5] * f_lat.reshape(bsz, n_lat, d)
    return x_lat
```

```python
import functools
import math

import numpy as np
import jax
import jax.numpy as jnp
from jax import lax
from jax.experimental import pallas as pl
from jax.experimental.pallas import tpu as pltpu

F32 = jnp.float32
BF16 = jnp.bfloat16

D_MODEL = 1024
DEPTH = 2
GRID_W = 64
ROPE_BASE = 10000.0
EPS = 1e-6
N_MOD = 6
CONV_PAD = (2, 1)

MLA_HEADS = 8
MLA_NOPE = 64
MLA_ROPE = 32
MLA_QK = MLA_NOPE + MLA_ROPE
MLA_V = 64
MLA_Q_RANK = 384
MLA_KV_RANK = 256
LRU_WIDTH = 512
LRU_C = 8.0

SWA_HEADS = 8
SWA_KV_HEADS = 2
SWA_GRP = SWA_HEADS // SWA_KV_HEADS
SWA_HD = 64
WINDOW = 128

SSD_HEADS = 8
SSD_HD = 64
SSD_INNER = SSD_HEADS * SSD_HD
SSD_STATE = 128
SSD_GROUPS = 2
SSD_CHUNK = 128
SSD_CONV_DIM = SSD_INNER + 2 * SSD_GROUPS * SSD_STATE

N_EXPERTS = 32
N_GROUPS = 4
EXPERTS_PER_GROUP = N_EXPERTS // N_GROUPS
TOP_K = 2
D_EXPERT = 512
MOE_BLK = 256

LANES = 128
HEAD_PAD = 128
VMEM_LIMIT = 56 * 1024 * 1024
NEG = -1e30


def _cparams(n_axes):
    return pltpu.CompilerParams(dimension_semantics=("arbitrary",) * n_axes, vmem_limit_bytes=VMEM_LIMIT)


def _row_tile(n, cap):
    t = cap
    while n % t:
        t //= 2
    return t


def _mm_kernel(x_ref, w_ref, o_ref):
    o_ref[...] = jnp.dot(x_ref[...].astype(BF16), w_ref[...], preferred_element_type=F32).astype(o_ref.dtype)


def matmul(x, w, out_dtype=F32, tm_cap=512):
    m, k = x.shape
    n = w.shape[1]
    tm = _row_tile(m, tm_cap)
    return pl.pallas_call(
        _mm_kernel,
        grid=(m // tm,),
        in_specs=[pl.BlockSpec((tm, k), lambda i: (i, 0)), pl.BlockSpec((k, n), lambda i: (0, 0))],
        out_specs=pl.BlockSpec((tm, n), lambda i: (i, 0)),
        out_shape=jax.ShapeDtypeStruct((m, n), out_dtype),
        compiler_params=_cparams(1),
    )(x, w.astype(BF16))


def _mm_precise_kernel(x_ref, w_ref, o_ref):
    x = x_ref[...]
    w = w_ref[...]
    xh = x.astype(BF16)
    xl = (x - xh.astype(F32)).astype(BF16)
    wh = w.astype(BF16)
    wl = (w - wh.astype(F32)).astype(BF16)
    acc = jnp.dot(xh, wh, preferred_element_type=F32)
    acc = acc + jnp.dot(xl, wh, preferred_element_type=F32)
    o_ref[...] = acc + jnp.dot(xh, wl, preferred_element_type=F32)


def matmul_precise(x, w, tm_cap=512, tn_cap=1536):
    m, k = x.shape
    n = w.shape[1]
    tm = _row_tile(m, tm_cap)
    tn = n if n <= tn_cap else _row_tile(n, tn_cap)
    return pl.pallas_call(
        _mm_precise_kernel,
        grid=(m // tm, n // tn),
        in_specs=[pl.BlockSpec((tm, k), lambda i, j: (i, 0)), pl.BlockSpec((k, tn), lambda i, j: (0, j))],
        out_specs=pl.BlockSpec((tm, tn), lambda i, j: (i, j)),
        out_shape=jax.ShapeDtypeStruct((m, n), F32),
        compiler_params=_cparams(2),
    )(x, w)


def _norm_mod_mm_kernel(x_ref, g_ref, sh_ref, sc_ref, w_ref, o_ref):
    x = x_ref[0]
    y = x * lax.rsqrt(jnp.mean(x * x, axis=-1, keepdims=True) + EPS)
    h = (y * g_ref[...]) * (1.0 + sc_ref[0]) + sh_ref[0]
    o_ref[0] = jnp.dot(h.astype(BF16), w_ref[...], preferred_element_type=F32).astype(o_ref.dtype)


def norm_mod_matmul(x, g, shift, scale, w, out_dtype=F32, tm_cap=512):
    b, l, d = x.shape
    n = w.shape[1]
    tm = _row_tile(l, tm_cap)
    return pl.pallas_call(
        _norm_mod_mm_kernel,
        grid=(b, l // tm),
        in_specs=[
            pl.BlockSpec((1, tm, d), lambda i, j: (i, j, 0)),
            pl.BlockSpec((1, d), lambda i, j: (0, 0)),
            pl.BlockSpec((1, 1, d), lambda i, j: (i, 0, 0)),
            pl.BlockSpec((1, 1, d), lambda i, j: (i, 0, 0)),
            pl.BlockSpec((d, n), lambda i, j: (0, 0)),
        ],
        out_specs=pl.BlockSpec((1, tm, n), lambda i, j: (i, j, 0)),
        out_shape=jax.ShapeDtypeStruct((b, l, n), out_dtype),
        compiler_params=_cparams(2),
    )(x, g.reshape(1, d), shift.reshape(b, 1, d), scale.reshape(b, 1, d), w.astype(BF16))


def _mm_residual_kernel(a_ref, w_ref, x_ref, gate_ref, o_ref):
    y = jnp.dot(a_ref[0].astype(BF16), w_ref[...], preferred_element_type=F32)
    o_ref[0] = x_ref[0] + gate_ref[0] * y


def matmul_residual(a, w, x, gate, tm_cap=512):
    b, l, k = a.shape
    d = w.shape[1]
    tm = _row_tile(l, tm_cap)
    return pl.pallas_call(
        _mm_residual_kernel,
        grid=(b, l // tm),
        in_specs=[
            pl.BlockSpec((1, tm, k), lambda i, j: (i, j, 0)),
            pl.BlockSpec((k, d), lambda i, j: (0, 0)),
            pl.BlockSpec((1, tm, d), lambda i, j: (i, j, 0)),
            pl.BlockSpec((1, 1, d), lambda i, j: (i, 0, 0)),
        ],
        out_specs=pl.BlockSpec((1, tm, d), lambda i, j: (i, j, 0)),
        out_shape=jax.ShapeDtypeStruct((b, l, d), F32),
        compiler_params=_cparams(2),
    )(a, w.astype(BF16), x, gate.reshape(b, 1, d))


def _mla_attn_kernel(q_ref, k_ref, v_ref, o_ref, acc_ref, m_ref, *, tk, n_chunks):
    tq = q_ref.shape[1]
    m_ref[...] = jnp.full(m_ref.shape, NEG, F32)
    acc_ref[...] = jnp.zeros(acc_ref.shape, F32)

    def body(c, carry):
        off = pl.multiple_of(c * tk, tk)
        vc = v_ref[0, pl.ds(off, tk), :]
        for hh in range(2):
            q = q_ref[0, :, hh * HEAD_PAD:(hh + 1) * HEAD_PAD]
            kc = k_ref[0, pl.ds(off, tk), hh * HEAD_PAD:(hh + 1) * HEAD_PAD]
            s = lax.dot_general(q, kc, (((1,), (1,)), ((), ())), preferred_element_type=F32)
            m_prev = m_ref[hh]
            m_new = jnp.maximum(m_prev, jnp.max(s, axis=-1, keepdims=True))
            p = jnp.exp(s - m_new)
            alpha = jnp.exp(m_prev - m_new)
            acc_ref[hh] = alpha * acc_ref[hh] + jnp.dot(p.astype(BF16), vc, preferred_element_type=F32)
            m_ref[hh] = m_new
        return carry

    lax.fori_loop(0, n_chunks, body, 0)
    out_a = acc_ref[0, :, :LANES] / acc_ref[0, :, LANES:]
    out_b = acc_ref[1, :, :LANES] / acc_ref[1, :, LANES:]
    lane = lax.broadcasted_iota(jnp.int32, (tq, LANES), 1)
    o_ref[0] = jnp.where(lane < MLA_V, out_a, out_b).astype(o_ref.dtype)


def mla_attention(q, k, v, tq_cap=512):
    b, lq, _ = q.shape
    lk = k.shape[1]
    n_pairs = MLA_HEADS // 2
    tq = _row_tile(lq, tq_cap)
    tk = 768 if lk % 768 == 0 else _row_tile(lk, 512)
    ones = jnp.ones((b, lk, n_pairs, LANES), BF16)
    v_aug = jnp.concatenate([v.astype(BF16).reshape(b, lk, n_pairs, LANES), ones], axis=-1)
    v_aug = v_aug.reshape(b, lk, n_pairs * 2 * LANES)
    kern = functools.partial(_mla_attn_kernel, tk=tk, n_chunks=lk // tk)
    return pl.pallas_call(
        kern,
        grid=(b, n_pairs, lq // tq),
        in_specs=[
            pl.BlockSpec((1, tq, 2 * HEAD_PAD), lambda i, h, j: (i, j, h)),
            pl.BlockSpec((1, lk, 2 * HEAD_PAD), lambda i, h, j: (i, 0, h)),
            pl.BlockSpec((1, lk, 2 * LANES), lambda i, h, j: (i, 0, h)),
        ],
        out_specs=pl.BlockSpec((1, tq, LANES), lambda i, h, j: (i, j, h)),
        out_shape=jax.ShapeDtypeStruct((b, lq, MLA_HEADS * MLA_V), BF16),
        scratch_shapes=[pltpu.VMEM((2, tq, 2 * LANES), F32), pltpu.VMEM((2, tq, 1), F32)],
        compiler_params=_cparams(3),
    )(q, k, v_aug)


def _lru_scan_kernel(a_ref, b_ref, h0_ref, h_ref, hlast_ref, carry_ref, *, reverse):
    t = pl.program_id(1)
    tt = a_ref.shape[1]

    @pl.when(t == 0)
    def _():
        carry_ref[...] = h0_ref[0]

    a = a_ref[0]
    b = b_ref[0]
    row = lax.broadcasted_iota(jnp.int32, a.shape, 0)
    s = 1
    while s < tt:
        if reverse:
            keep = row < tt - s
            a_s = pltpu.roll(a, tt - s, axis=0)
            b_s = pltpu.roll(b, tt - s, axis=0)
        else:
            keep = row >= s
            a_s = pltpu.roll(a, s, axis=0)
            b_s = pltpu.roll(b, s, axis=0)
        b = jnp.where(keep, a * b_s + b, b)
        a = jnp.where(keep, a * a_s, a)
        s *= 2
    h = a * carry_ref[...] + b
    h_ref[0] = h
    last = h[0:1] if reverse else h[tt - 1:tt]
    carry_ref[...] = last
    hlast_ref[0] = last


def lru_scan(a, b, h0, reverse, tt_cap=256):
    bsz, l, c = a.shape
    tt = _row_tile(l, tt_cap)
    nt = l // tt
    tmap = (lambda i, j: (i, nt - 1 - j, 0)) if reverse else (lambda i, j: (i, j, 0))
    h, hl = pl.pallas_call(
        functools.partial(_lru_scan_kernel, reverse=reverse),
        grid=(bsz, nt),
        in_specs=[
            pl.BlockSpec((1, tt, c), tmap),
            pl.BlockSpec((1, tt, c), tmap),
            pl.BlockSpec((1, 1, c), lambda i, j: (i, 0, 0)),
        ],
        out_specs=[pl.BlockSpec((1, tt, c), tmap), pl.BlockSpec((1, 1, c), lambda i, j: (i, 0, 0))],
        out_shape=[jax.ShapeDtypeStruct((bsz, l, c), F32), jax.ShapeDtypeStruct((bsz, 1, c), F32)],
        scratch_shapes=[pltpu.VMEM((1, c), F32)],
        compiler_params=_cparams(2),
    )(a, b, h0.reshape(bsz, 1, c))
    return h, hl.reshape(bsz, c)


def _swa_kernel(sink_ref, q_ref, kp_ref, km_ref, kn_ref, vp_ref, vm_ref, vn_ref, kc_ref, vc_ref, o_ref, *, seq):
    g = pl.program_id(1)
    qt = pl.program_id(2)
    tq = q_ref.shape[3]
    rows = SWA_GRP * tq
    q = q_ref[0, 0].reshape(rows, SWA_HD)
    kb = jnp.concatenate([kp_ref[0, 0], km_ref[0, 0], kn_ref[0, 0]], axis=0)
    vb = jnp.concatenate([vp_ref[0, 0], vm_ref[0, 0], vn_ref[0, 0]], axis=0)
    nb = tq + 2 * WINDOW
    nt = (((1,), (1,)), ((), ()))
    s_band = lax.dot_general(q, kb, nt, preferred_element_type=F32)
    s_ctx = lax.dot_general(q, kc_ref[0, 0], nt, preferred_element_type=F32)
    qi = lax.broadcasted_iota(jnp.int32, (rows, nb), 0) & (tq - 1)
    kj = lax.broadcasted_iota(jnp.int32, (rows, nb), 1) - WINDOW
    kpos = qt * tq + kj
    valid = (jnp.abs(kj - qi) <= WINDOW) & (kpos >= 0) & (kpos < seq)
    s_band = jnp.where(valid, s_band, NEG)
    rid = lax.broadcasted_iota(jnp.int32, (rows, 1), 0)
    sink = jnp.full((rows, 1), sink_ref[g * SWA_GRP], F32)
    for j in range(1, SWA_GRP):
        sink = jnp.where(rid >= j * tq, sink_ref[g * SWA_GRP + j], sink)
    m = jnp.maximum(jnp.maximum(jnp.max(s_band, axis=-1, keepdims=True), jnp.max(s_ctx, axis=-1, keepdims=True)), sink)
    p_band = jnp.exp(s_band - m)
    p_ctx = jnp.exp(s_ctx - m)
    denom = jnp.sum(p_band, axis=-1, keepdims=True) + jnp.sum(p_ctx, axis=-1, keepdims=True) + jnp.exp(sink - m)
    o = jnp.dot(p_band.astype(BF16), vb, preferred_element_type=F32)
    o = o + jnp.dot(p_ctx.astype(BF16), vc_ref[0, 0], preferred_element_type=F32)
    o_ref[0, 0] = (o / denom).reshape(SWA_GRP, tq, SWA_HD).astype(o_ref.dtype)


def window_attention(q, k, v, k_ctx, v_ctx, sink, tq=256):
    b, s, hq, hd = q.shape
    hkv = k.shape[2]
    lc = k_ctx.shape[1]
    wb = tq // WINDOW
    n_wblk = s // WINDOW
    qh = jnp.transpose(q.astype(BF16), (0, 2, 1, 3)).reshape(b, hkv, SWA_GRP, s, hd)
    kh, vh, kch, vch = (jnp.transpose(t.astype(BF16), (0, 2, 1, 3)) for t in (k, v, k_ctx, v_ctx))
    prev_map = lambda i, g, j: (i, g, jnp.maximum(j * wb - 1, 0), 0)
    main_map = lambda i, g, j: (i, g, j, 0)
    next_map = lambda i, g, j: (i, g, jnp.minimum((j + 1) * wb, n_wblk - 1), 0)
    ctx_map = lambda i, g, j: (i, g, 0, 0)
    band_specs = [pl.BlockSpec((1, 1, WINDOW, hd), prev_map), pl.BlockSpec((1, 1, tq, hd), main_map),
                  pl.BlockSpec((1, 1, WINDOW, hd), next_map)]
    out = pl.pallas_call(
        functools.partial(_swa_kernel, seq=s),
        grid=(b, hkv, s // tq),
        in_specs=[pl.BlockSpec(memory_space=pltpu.SMEM),
                  pl.BlockSpec((1, 1, SWA_GRP, tq, hd), lambda i, g, j: (i, g, 0, j, 0))]
                 + band_specs + band_specs
                 + [pl.BlockSpec((1, 1, lc, hd), ctx_map), pl.BlockSpec((1, 1, lc, hd), ctx_map)],
        out_specs=pl.BlockSpec((1, 1, SWA_GRP, tq, hd), lambda i, g, j: (i, g, 0, j, 0)),
        out_shape=jax.ShapeDtypeStruct((b, hkv, SWA_GRP, s, hd), BF16),
        compiler_params=_cparams(3),
    )(sink.astype(F32), qh, kh, kh, kh, vh, vh, vh, kch, vch)
    return jnp.transpose(out.reshape(b, hq, s, hd), (0, 2, 1, 3)).reshape(b, s, hq * hd)


def _split3(x):
    hi = x.astype(BF16)
    r1 = x - hi.astype(F32)
    mid = r1.astype(BF16)
    lo = (r1 - mid.astype(F32)).astype(BF16)
    return hi, mid, lo


def _softplus(x):
    return jnp.maximum(x, 0.0) + jnp.log(1.0 + jnp.exp(-jnp.abs(x)))


def _ssd_kernel(xs_ref, dtc_ref, dtr_ref, bmt_ref, cm_ref, h0_ref, bias_c_ref, bias_r_ref, a_c_ref, a_r_ref,
                dsk_ref, *out_refs, reverse, want_y):
    if want_y:
        y_ref, hfin_ref, st_ref = out_refs
    else:
        hfin_ref, st_ref = out_refs
    c = pl.program_id(1)
    nc = pl.num_programs(1)
    T = SSD_CHUNK

    @pl.when(c == 0)
    def _():
        st_ref[...] = h0_ref[0]

    dt_c = _softplus(dtc_ref[0] + bias_r_ref[...])
    dt_r = _softplus(dtr_ref[0] + bias_c_ref[...])
    a_c = dt_c * a_r_ref[...]
    a_r = dt_r * a_c_ref[...]
    ii = lax.broadcasted_iota(jnp.int32, (T, T), 0)
    jj = lax.broadcasted_iota(jnp.int32, (T, T), 1)
    causal = (jj >= ii) if reverse else (jj <= ii)
    tri = jnp.where(causal, 1.0, 0.0).astype(BF16)
    tri_t = jnp.where((ii >= jj) if reverse else (ii <= jj), 1.0, 0.0).astype(BF16)
    cum_c = sum(jnp.dot(tri, p, preferred_element_type=F32) for p in _split3(a_c))
    cum_r = sum(jnp.dot(p, tri_t, preferred_element_type=F32) for p in _split3(a_r))
    edge = 0 if reverse else T - 1
    tot_c = cum_c[edge:edge + 1, :]
    cm = cm_ref[0]
    xs = xs_ref[0]
    ys = []
    for g in range(SSD_GROUPS):
        cm_g = cm[:, g * SSD_STATE:(g + 1) * SSD_STATE]
        bmt_g = bmt_ref[0, g]
        cb = jnp.dot(cm_g, bmt_g, preferred_element_type=F32) if want_y else None
        for hh in range(SSD_HEADS // SSD_GROUPS):
            h = g * (SSD_HEADS // SSD_GROUPS) + hh
            col = cum_c[:, h:h + 1]
            rw = cum_r[h:h + 1, :]
            x_h = xs[:, h * SSD_HD:(h + 1) * SSD_HD]
            xdt = (x_h * dt_c[:, h:h + 1]).astype(BF16)
            st = st_ref[h]
            if want_y:
                decay = jnp.where(causal, jnp.exp(jnp.minimum(col - rw, 0.0)), 0.0)
                y_diag = jnp.dot((cb * decay).astype(BF16), xdt, preferred_element_type=F32)
                y_off = jnp.dot(cm_g, st.astype(BF16), preferred_element_type=F32) * jnp.exp(col)
                ys.append(y_diag + y_off + dsk_ref[:, h * SSD_HD:(h + 1) * SSD_HD] * x_h)
            to_edge = jnp.exp(tot_c[:, h:h + 1] - rw)
            upd = jnp.dot((bmt_g.astype(F32) * to_edge).astype(BF16), xdt, preferred_element_type=F32)
            st_ref[h] = jnp.exp(tot_c[:, h:h + 1]) * st + upd
    if want_y:
        y_ref[0] = jnp.concatenate(ys, axis=-1)

    @pl.when(c == nc - 1)
    def _():
        hfin_ref[0] = st_ref[...]


def ssd_scan(xs, dt_raw, bm, cm, a_log, dt_bias, d_skip, h0, reverse, want_y):
    b, l, _ = xs.shape
    nc = l // SSD_CHUNK
    hds, n, p = SSD_HEADS, SSD_STATE, SSD_HD
    a_neg = -jnp.exp(a_log.astype(F32))
    bmt = jnp.transpose(bm.astype(BF16).reshape(b, l, SSD_GROUPS, n), (0, 2, 3, 1))
    dtr = jnp.transpose(dt_raw, (0, 2, 1))
    cmap = (lambda i, j: (i, nc - 1 - j, 0)) if reverse else (lambda i, j: (i, j, 0))
    cmap_t = (lambda i, j: (i, 0, nc - 1 - j)) if reverse else (lambda i, j: (i, 0, j))
    cmap_bt = (lambda i, j: (i, 0, 0, nc - 1 - j)) if reverse else (lambda i, j: (i, 0, 0, j))
    full2 = lambda i, j: (0, 0)
    st_map = lambda i, j: (i, 0, 0, 0)
    y_spec = [pl.BlockSpec((1, SSD_CHUNK, hds * p), cmap)] if want_y else []
    y_shape = [jax.ShapeDtypeStruct((b, l, hds * p), F32)] if want_y else []
    outs = pl.pallas_call(
        functools.partial(_ssd_kernel, reverse=reverse, want_y=want_y),
        grid=(b, nc),
        in_specs=[
            pl.BlockSpec((1, SSD_CHUNK, hds * p), cmap),
            pl.BlockSpec((1, SSD_CHUNK, hds), cmap),
            pl.BlockSpec((1, hds, SSD_CHUNK), cmap_t),
            pl.BlockSpec((1, SSD_GROUPS, n, SSD_CHUNK), cmap_bt),
            pl.BlockSpec((1, SSD_CHUNK, SSD_GROUPS * n), cmap),
            pl.BlockSpec((1, hds, n, p), st_map),
            pl.BlockSpec((hds, 1), full2),
            pl.BlockSpec((1, hds), full2),
            pl.BlockSpec((hds, 1), full2),
            pl.BlockSpec((1, hds), full2),
            pl.BlockSpec((1, hds * p), full2),
        ],
        out_specs=y_spec + [pl.BlockSpec((1, hds, n, p), st_map)],
        out_shape=y_shape + [jax.ShapeDtypeStruct((b, hds, n, p), F32)],
        scratch_shapes=[pltpu.VMEM((hds, n, p), F32)],
        compiler_params=_cparams(2),
    )(xs, dt_raw, dtr, bmt, cm.astype(BF16), h0, dt_bias.astype(F32).reshape(hds, 1),
      dt_bias.astype(F32).reshape(1, hds), a_neg.reshape(hds, 1), a_neg.reshape(1, hds),
      jnp.repeat(d_skip.astype(F32), p).reshape(1, hds * p))
    return (outs[0], outs[1]) if want_y else (None, outs[0])


def _expert_kernel(be_ref, nu_ref, x_ref, sw_ref, wg_ref, wu_ref, wd_ref, o_ref):
    i = pl.program_id(0)

    @pl.when(i < nu_ref[0])
    def _():
        x = x_ref[...]
        gate = jnp.dot(x, wg_ref[0], preferred_element_type=F32)
        up = jnp.dot(x, wu_ref[0], preferred_element_type=F32)
        act = (gate * jax.nn.sigmoid(gate) * up).astype(BF16)
        o_ref[...] = jnp.dot(act, wd_ref[0], preferred_element_type=F32) * sw_ref[...]

    @pl.when(i >= nu_ref[0])
    def _():
        o_ref[...] = jnp.zeros(o_ref.shape, F32)


def expert_blocks(xb, slot_w, block_expert, n_used, w_gate, w_up, w_down):
    rows, d = xb.shape
    n_blocks = rows // MOE_BLK
    de = w_gate.shape[-1]
    grid_spec = pltpu.PrefetchScalarGridSpec(
        num_scalar_prefetch=2,
        grid=(n_blocks,),
        in_specs=[
            pl.BlockSpec((MOE_BLK, d), lambda i, be, nu: (i, 0)),
            pl.BlockSpec((MOE_BLK, 1), lambda i, be, nu: (i, 0)),
            pl.BlockSpec((1, d, de), lambda i, be, nu: (be[i], 0, 0)),
            pl.BlockSpec((1, d, de), lambda i, be, nu: (be[i], 0, 0)),
            pl.BlockSpec((1, de, d), lambda i, be, nu: (be[i], 0, 0)),
        ],
        out_specs=pl.BlockSpec((MOE_BLK, d), lambda i, be, nu: (i, 0)),
    )
    return pl.pallas_call(
        _expert_kernel,
        grid_spec=grid_spec,
        out_shape=jax.ShapeDtypeStruct((rows, d), F32),
        compiler_params=_cparams(1),
    )(block_expert, n_used, xb, slot_w, w_gate.astype(BF16), w_up.astype(BF16), w_down.astype(BF16))


def moe_ffn(h, router_w, router_b, w_gate, w_up, w_down):
    n, d = h.shape
    logits = matmul_precise(h, router_w)
    scores = jax.nn.sigmoid(logits.astype(F32))
    biased = (scores + router_b.astype(F32)).reshape(n, N_GROUPS, EXPERTS_PER_GROUP)
    group_score = lax.top_k(biased, TOP_K)[0].sum(-1)
    g_sel = jnp.argmax(group_score, axis=-1)
    in_group = jnp.take_along_axis(biased, g_sel[:, None, None], axis=1)[:, 0]
    _, local = lax.top_k(in_group, TOP_K)
    e_sel = g_sel[:, None] * EXPERTS_PER_GROUP + local
    s_sel = jnp.take_along_axis(scores, e_sel, axis=1)
    w_sel = s_sel / jnp.sum(s_sel, axis=-1, keepdims=True)

    m = n * TOP_K
    n_blocks = -(-m // MOE_BLK) + N_EXPERTS
    e_flat = e_sel.reshape(m).astype(jnp.int32)
    tok_flat = jnp.repeat(jnp.arange(n, dtype=jnp.int32), TOP_K)
    w_flat = w_sel.reshape(m)
    onehot = (e_flat[:, None] == jnp.arange(N_EXPERTS, dtype=jnp.int32)[None, :]).astype(jnp.int32)
    rank = jnp.take_along_axis(jnp.cumsum(onehot, axis=0) - onehot, e_flat[:, None], axis=1)[:, 0]
    counts = jnp.sum(onehot, axis=0)
    padded = (counts + MOE_BLK - 1) // MOE_BLK * MOE_BLK
    pad_end = jnp.cumsum(padded)
    pad_start = pad_end - padded
    dest = pad_start[e_flat] + rank
    slot_tok = jnp.full((n_blocks * MOE_BLK,), n, jnp.int32).at[dest].set(tok_flat)
    slot_w = jnp.zeros((n_blocks * MOE_BLK,), F32).at[dest].set(w_flat)
    block_expert = jnp.minimum(jnp.searchsorted(pad_end, jnp.arange(n_blocks) * MOE_BLK, side='right'),
                               N_EXPERTS - 1).astype(jnp.int32)
    n_used = (pad_end[-1] // MOE_BLK).astype(jnp.int32).reshape(1)
    h_pad = jnp.concatenate([h.astype(BF16), jnp.zeros((1, d), BF16)], axis=0)
    xb = h_pad[slot_tok]
    yb = expert_blocks(xb, slot_w[:, None], block_expert, n_used, w_gate, w_up, w_down)
    yd = yb[dest].reshape(n, TOP_K, d)
    return yd[:, 0] + yd[:, 1]


def rmsnorm(x, g):
    xf = x.astype(F32)
    y = xf * lax.rsqrt(jnp.mean(xf * xf, axis=-1, keepdims=True) + EPS)
    return y * g.astype(F32)


def grid_positions(n):
    rows = n // GRID_W
    row = jnp.repeat(jnp.arange(rows, dtype=F32), GRID_W)
    col = jnp.tile(jnp.arange(GRID_W, dtype=F32), rows)
    return row, col


def rope_1d(x, pos):
    f = x.shape[-1] // 2
    inv = ROPE_BASE ** (-jnp.arange(f, dtype=F32) / f)
    ang = pos[:, None] * inv[None, :]
    cos = jnp.cos(ang)[None, :, None, :]
    sin = jnp.sin(ang)[None, :, None, :]
    x1 = x[..., :f]
    x2 = x[..., f:]
    return jnp.concatenate([x1 * cos - x2 * sin, x1 * sin + x2 * cos], axis=-1)


def rope_2d(x, row, col):
    half = x.shape[-1] // 2
    return jnp.concatenate([rope_1d(x[..., :half], row), rope_1d(x[..., half:], col)], axis=-1)


def centred_dwconv(x, w, b):
    y = lax.conv_general_dilated(x, w[:, None, :].astype(x.dtype), window_strides=(1,), padding=[CONV_PAD],
                                 dimension_numbers=('NWC', 'WIO', 'NWC'), feature_group_count=x.shape[-1])
    return y + b.astype(x.dtype)


def block_diag_dense(w):
    nb, di, do = w.shape
    eye = jnp.eye(nb, dtype=w.dtype)
    return jnp.einsum('hij,hk->hikj', w, eye).reshape(nb * di, nb * do)


def mixer_ab(x_lat, x_ctx, g1, mod_l, mod_c, row, col, need_ctx, w_in, w_out, q_norm_g, w_q_up, kv_norm_g, w_kv_up,
             q_head_g, k_head_g, conv_w, conv_b, w_a, b_a, w_x, b_x, lam):
    bsz, n_lat, _ = x_lat.shape
    n_ctx = x_ctx.shape[1]
    cuts = np.cumsum([MLA_Q_RANK, MLA_KV_RANK, MLA_ROPE, LRU_WIDTH]).tolist()
    p_l = norm_mod_matmul(x_lat, g1, mod_l[:, 0], mod_l[:, 1], w_in)
    p_c = norm_mod_matmul(x_ctx, g1, mod_c[:, 0], mod_c[:, 1], w_in)
    cq_l, ckv_l, kr_l, xr_l, gr_l = jnp.split(p_l, cuts, axis=-1)
    cq_c, ckv_c, kr_c, xr_c, gr_c = jnp.split(p_c, cuts, axis=-1)

    def up(t, g, w):
        b, n, r = t.shape
        return matmul(rmsnorm(t, g).reshape(b * n, r).astype(BF16), w).reshape(b, n, -1)

    def heads_q(cq):
        b, n = cq.shape[:2]
        return rmsnorm(up(cq, q_norm_g, w_q_up).reshape(b, n, MLA_HEADS, MLA_QK), q_head_g)

    def heads_kv(ckv, kr):
        b, n = ckv.shape[:2]
        kv = up(ckv, kv_norm_g, w_kv_up).reshape(b, n, MLA_HEADS, MLA_NOPE + MLA_V)
        k_rope = jnp.broadcast_to(kr[:, :, None, :], (b, n, MLA_HEADS, MLA_ROPE))
        k = rmsnorm(jnp.concatenate([kv[..., :MLA_NOPE], k_rope], axis=-1), k_head_g)
        return k, kv[..., MLA_NOPE:]

    def rope_tail(t):
        return jnp.concatenate([t[..., :MLA_NOPE], rope_2d(t[..., MLA_NOPE:], row, col)], axis=-1)

    def pad_heads(t, scale=1.0):
        b, n = t.shape[:2]
        t = jnp.pad(t * scale, ((0, 0), (0, 0), (0, 0), (0, HEAD_PAD - MLA_QK)))
        return t.astype(BF16).reshape(b, n, MLA_HEADS * HEAD_PAD)

    k_c, v_c = heads_kv(ckv_c, kr_c)
    k_l, v_l = heads_kv(ckv_l, kr_l)
    q_l = rope_tail(heads_q(cq_l))
    k_l = rope_tail(k_l)
    k_all = jnp.concatenate([pad_heads(k_c), pad_heads(k_l)], axis=1)
    v_all = jnp.concatenate([v_c, v_l], axis=1).reshape(bsz, n_ctx + n_lat, MLA_HEADS * MLA_V)
    attn_l = mla_attention(pad_heads(q_l, MLA_QK ** -0.5), k_all, v_all)

    xc_l = centred_dwconv(xr_l, conv_w, conv_b)
    xc_c = centred_dwconv(xr_c, conv_w, conv_b)
    w_gates = jnp.concatenate([block_diag_dense(w_a[0]), block_diag_dense(w_x[0]),
                               block_diag_dense(w_a[1]), block_diag_dense(w_x[1])], axis=1)

    def gates(xc):
        b, n, c = xc.shape
        return matmul(xc.reshape(b * n, c), w_gates).reshape(b, n, 4, c)

    gt_l, gt_c = gates(xc_l), gates(xc_c)

    def scan_inputs(xc, gt, d):
        r = jax.nn.sigmoid(gt[:, :, 2 * d] + b_a[d])
        i = jax.nn.sigmoid(gt[:, :, 2 * d + 1] + b_x[d])
        log_a = -LRU_C * r * jax.nn.softplus(-lam[d].astype(F32))
        return jnp.exp(log_a), jnp.sqrt(-jnp.expm1(2.0 * log_a)) * (i * xc)

    zero = jnp.zeros((bsz, LRU_WIDTH), F32)
    hs_l, hs_c = [], []
    for d, rev in enumerate((False, True)):
        h_c, h_end = lru_scan(*scan_inputs(xc_c, gt_c, d), zero, rev)
        h_l, _ = lru_scan(*scan_inputs(xc_l, gt_l, d), h_end, rev)
        hs_l.append(h_l)
        hs_c.append(h_c)
    lru_l = (hs_l[0] + hs_l[1]) * jax.nn.gelu(gr_l)
    y_in_l = jnp.concatenate([attn_l, lru_l.astype(BF16)], axis=-1)
    x_lat = matmul_residual(y_in_l, w_out, x_lat, mod_l[:, 2])
    if not need_ctx:
        return x_lat, None
    attn_c = mla_attention(pad_heads(heads_q(cq_c), MLA_QK ** -0.5), pad_heads(k_c),
                           v_c.reshape(bsz, n_ctx, MLA_HEADS * MLA_V))
    lru_c = (hs_c[0] + hs_c[1]) * jax.nn.gelu(gr_c)
    y_in_c = jnp.concatenate([attn_c, lru_c.astype(BF16)], axis=-1)
    x_ctx = matmul_residual(y_in_c, w_out, x_ctx, mod_c[:, 2])
    return x_lat, x_ctx


def mixer_cd(x_lat, x_ctx, g1, mod_l, mod_c, row, col, need_ctx, w_in, w_out, q_head_g, k_head_g, sink,
             conv_w, conv_b, a_log, dt_bias, d_skip, norm_g):
    q_w = SWA_HEADS * SWA_HD
    kv_w = SWA_KV_HEADS * SWA_HD
    cuts = np.cumsum([q_w, kv_w, kv_w, SSD_INNER, SSD_CONV_DIM]).tolist()
    p_l = norm_mod_matmul(x_lat, g1, mod_l[:, 0], mod_l[:, 1], w_in)
    p_c = norm_mod_matmul(x_ctx, g1, mod_c[:, 0], mod_c[:, 1], w_in)
    q_l, k_l, v_l, z_l, xbc_l, dt_l = jnp.split(p_l, cuts, axis=-1)
    q_c, k_c, v_c, z_c, xbc_c, dt_c = jnp.split(p_c, cuts, axis=-1)
    bsz, n_lat = x_lat.shape[:2]
    n_ctx = x_ctx.shape[1]

    def heads(t, nh):
        return t.reshape(t.shape[0], t.shape[1], nh, SWA_HD)

    k_ch = rmsnorm(heads(k_c, SWA_KV_HEADS), k_head_g)
    v_ch = heads(v_c, SWA_KV_HEADS)
    q_lh = rope_2d(rmsnorm(heads(q_l, SWA_HEADS), q_head_g), row, col) * SWA_HD ** -0.5
    k_lh = rope_2d(rmsnorm(heads(k_l, SWA_KV_HEADS), k_head_g), row, col)
    attn_l = window_attention(q_lh, k_lh, heads(v_l, SWA_KV_HEADS), k_ch, v_ch, sink)

    def ssd_prep(xbc):
        u = jax.nn.silu(centred_dwconv(xbc, conv_w, conv_b))
        return (u[..., :SSD_INNER], u[..., SSD_INNER:SSD_INNER + SSD_GROUPS * SSD_STATE],
                u[..., SSD_INNER + SSD_GROUPS * SSD_STATE:])

    xs_l, bm_l, cm_l = ssd_prep(xbc_l)
    xs_c, bm_c, cm_c = ssd_prep(xbc_c)
    h0 = jnp.zeros((bsz, SSD_HEADS, SSD_STATE, SSD_HD), F32)
    ys_l, ys_c = [], []
    for d in range(2):
        sl = slice(d * SSD_HEADS, (d + 1) * SSD_HEADS)
        y_c, h_c = ssd_scan(xs_c, dt_c[..., sl], bm_c, cm_c, a_log[d], dt_bias[d], d_skip[d], h0, d == 1, need_ctx)
        y_l, _ = ssd_scan(xs_l, dt_l[..., sl], bm_l, cm_l, a_log[d], dt_bias[d], d_skip[d], h_c, d == 1, True)
        ys_l.append(y_l)
        ys_c.append(y_c)

    def gated_out(ys, z):
        return rmsnorm((ys[0] + ys[1]) * jax.nn.silu(z), norm_g)

    y_in_l = jnp.concatenate([attn_l, gated_out(ys_l, z_l).astype(BF16)], axis=-1)
    x_lat = matmul_residual(y_in_l, w_out, x_lat, mod_l[:, 2])
    if not need_ctx:
        return x_lat, None
    raise NotImplementedError("context output of the windowed/SSD mixer is only needed for deeper stacks")


def kernel(x, c, ctx, c_ctx, ada_w, ada_b, norm1_g, norm2_g, ab_w_in, ab_w_out, mla_q_norm_g, mla_w_q_up,
           mla_kv_norm_g, mla_w_kv_up, mla_q_head_g, mla_k_head_g, lru_conv_w, lru_conv_b, lru_w_a, lru_b_a,
           lru_w_x, lru_b_x, lru_lambda, cd_w_in, cd_w_out, swa_q_head_g, swa_k_head_g, swa_sink, ssd_conv_w,
           ssd_conv_b, ssd_a_log, ssd_dt_bias, ssd_d, ssd_norm_g, router_w, router_b, moe_w_gate, moe_w_up,
           moe_w_down):
    bsz, n_lat, d = x.shape
    n_ctx = ctx.shape[1]
    row, col = grid_positions(n_lat)
    x_lat, x_ctx = x, ctx
    cc = jnp.concatenate([c, c_ctx[None, :], jnp.zeros((8 - (bsz + 1) % 8, d), F32)], axis=0)
    for layer in range(DEPTH):
        need_ctx = layer < DEPTH - 1
        mod = (matmul_precise(jax.nn.silu(cc), ada_w[layer])[:bsz + 1] + ada_b[layer]).reshape(bsz + 1, N_MOD, d)
        mod_l = mod[:bsz]
        mod_c = jnp.broadcast_to(mod[bsz:], (bsz, N_MOD, d))
        i = layer // 2
        if layer % 2 == 0:
            x_lat, x_ctx = mixer_ab(x_lat, x_ctx, norm1_g[layer], mod_l, mod_c, row, col, need_ctx, ab_w_in[i],
                                    ab_w_out[i], mla_q_norm_g[i], mla_w_q_up[i], mla_kv_norm_g[i], mla_w_kv_up[i],
                                    mla_q_head_g[i], mla_k_head_g[i], lru_conv_w[i], lru_conv_b[i],
                                    lru_w_a[i], lru_b_a[i], lru_w_x[i], lru_b_x[i], lru_lambda[i])
        else:
            x_lat, x_ctx = mixer_cd(x_lat, x_ctx, norm1_g[layer], mod_l, mod_c, row, col, need_ctx, cd_w_in[i],
                                    cd_w_out[i], swa_q_head_g[i], swa_k_head_g[i], swa_sink[i], ssd_conv_w[i],
                                    ssd_conv_b[i], ssd_a_log[i], ssd_dt_bias[i], ssd_d[i], ssd_norm_g[i])

        def mod2(xx, mm):
            return rmsnorm(xx, norm2_g[layer]) * (1.0 + mm[:, 4][:, None, :]) + mm[:, 3][:, None, :]

        h2_lat = mod2(x_lat, mod_l).reshape(bsz * n_lat, d)
        if need_ctx:
            h2_ctx = mod2(x_ctx, mod_c).reshape(bsz * n_ctx, d)
            f = moe_ffn(jnp.concatenate([h2_lat, h2_ctx], axis=0), router_w, router_b,
                        moe_w_gate[layer], moe_w_up[layer], moe_w_down[layer])
            x_ctx = x_ctx + mod_c[:, 5][:, None, :] * f[bsz * n_lat:].reshape(bsz, n_ctx, d)
            f_lat = f[:bsz * n_lat]
        else:
            f_lat = moe_ffn(h2_lat, router_w, router_b, moe_w_gate[layer], moe_w_up[layer], moe_w_down[layer])
        x_lat = x_lat + mod_l[:, 5][:, None, :] * f_lat.reshape(bsz, n_lat, d)
    return x_lat
```

```python
import functools
import math

import numpy as np
import jax
import jax.numpy as jnp
from jax import lax
from jax.experimental import pallas as pl
from jax.experimental.pallas import tpu as pltpu

F32 = jnp.float32
BF16 = jnp.bfloat16

D_MODEL = 1024
DEPTH = 2
GRID_W = 64
ROPE_BASE = 10000.0
EPS = 1e-6
N_MOD = 6
CONV_PAD = (2, 1)

MLA_HEADS = 8
MLA_NOPE = 64
MLA_ROPE = 32
MLA_QK = MLA_NOPE + MLA_ROPE
MLA_V = 64
MLA_Q_RANK = 384
MLA_KV_RANK = 256
LRU_WIDTH = 512
LRU_C = 8.0

SWA_HEADS = 8
SWA_KV_HEADS = 2
SWA_GRP = SWA_HEADS // SWA_KV_HEADS
SWA_HD = 64
WINDOW = 128

SSD_HEADS = 8
SSD_HD = 64
SSD_INNER = SSD_HEADS * SSD_HD
SSD_STATE = 128
SSD_GROUPS = 2
SSD_CHUNK = 128
SSD_CONV_DIM = SSD_INNER + 2 * SSD_GROUPS * SSD_STATE

N_EXPERTS = 32
N_GROUPS = 4
EXPERTS_PER_GROUP = N_EXPERTS // N_GROUPS
TOP_K = 2
D_EXPERT = 512
MOE_BLK = 256

LANES = 128
HEAD_PAD = 128
VMEM_LIMIT = 56 * 1024 * 1024
NEG = -1e30


def _cparams(n_axes):
    return pltpu.CompilerParams(dimension_semantics=("arbitrary",) * n_axes, vmem_limit_bytes=VMEM_LIMIT)


def _row_tile(n, cap):
    t = cap
    while n % t:
        t //= 2
    return t


def _mm_kernel(x_ref, w_ref, o_ref):
    o_ref[...] = jnp.dot(x_ref[...].astype(BF16), w_ref[...], preferred_element_type=F32).astype(o_ref.dtype)


def matmul(x, w, out_dtype=F32, tm_cap=512):
    m, k = x.shape
    n = w.shape[1]
    tm = _row_tile(m, tm_cap)
    return pl.pallas_call(
        _mm_kernel,
        grid=(m // tm,),
        in_specs=[pl.BlockSpec((tm, k), lambda i: (i, 0)), pl.BlockSpec((k, n), lambda i: (0, 0))],
        out_specs=pl.BlockSpec((tm, n), lambda i: (i, 0)),
        out_shape=jax.ShapeDtypeStruct((m, n), out_dtype),
        compiler_params=_cparams(1),
        name="mm",
    )(x, w.astype(BF16))


def _mm_precise_kernel(x_ref, w_ref, o_ref):
    x = x_ref[...]
    w = w_ref[...]
    xh = x.astype(BF16)
    xl = (x - xh.astype(F32)).astype(BF16)
    wh = w.astype(BF16)
    wl = (w - wh.astype(F32)).astype(BF16)
    acc = jnp.dot(xh, wh, preferred_element_type=F32)
    acc = acc + jnp.dot(xl, wh, preferred_element_type=F32)
    o_ref[...] = acc + jnp.dot(xh, wl, preferred_element_type=F32)


def matmul_precise(x, w, tm_cap=512, tn_cap=1536):
    m, k = x.shape
    n = w.shape[1]
    tm = _row_tile(m, tm_cap)
    tn = n if n <= tn_cap else _row_tile(n, tn_cap)
    return pl.pallas_call(
        _mm_precise_kernel,
        grid=(m // tm, n // tn),
        in_specs=[pl.BlockSpec((tm, k), lambda i, j: (i, 0)), pl.BlockSpec((k, tn), lambda i, j: (0, j))],
        out_specs=pl.BlockSpec((tm, tn), lambda i, j: (i, j)),
        out_shape=jax.ShapeDtypeStruct((m, n), F32),
        compiler_params=_cparams(2),
        name="mm_precise",
    )(x, w)


def _norm_mod_mm_kernel(x_ref, g_ref, sh_ref, sc_ref, w_ref, o_ref):
    x = x_ref[0]
    y = x * lax.rsqrt(jnp.mean(x * x, axis=-1, keepdims=True) + EPS)
    h = (y * g_ref[...]) * (1.0 + sc_ref[0]) + sh_ref[0]
    o_ref[0] = jnp.dot(h.astype(BF16), w_ref[...], preferred_element_type=F32).astype(o_ref.dtype)


def norm_mod_matmul(x, g, shift, scale, w, out_dtype=F32, tm_cap=512):
    b, l, d = x.shape
    n = w.shape[1]
    tm = _row_tile(l, tm_cap)
    return pl.pallas_call(
        _norm_mod_mm_kernel,
        grid=(b, l // tm),
        in_specs=[
            pl.BlockSpec((1, tm, d), lambda i, j: (i, j, 0)),
            pl.BlockSpec((1, d), lambda i, j: (0, 0)),
            pl.BlockSpec((1, 1, d), lambda i, j: (i, 0, 0)),
            pl.BlockSpec((1, 1, d), lambda i, j: (i, 0, 0)),
            pl.BlockSpec((d, n), lambda i, j: (0, 0)),
        ],
        out_specs=pl.BlockSpec((1, tm, n), lambda i, j: (i, j, 0)),
        out_shape=jax.ShapeDtypeStruct((b, l, n), out_dtype),
        compiler_params=_cparams(2),
        name="norm_mod_mm",
    )(x, g.reshape(1, d), shift.reshape(b, 1, d), scale.reshape(b, 1, d), w.astype(BF16))


def _mm_residual_kernel(a_ref, w_ref, x_ref, gate_ref, o_ref):
    y = jnp.dot(a_ref[0].astype(BF16), w_ref[...], preferred_element_type=F32)
    o_ref[0] = x_ref[0] + gate_ref[0] * y


def matmul_residual(a, w, x, gate, tm_cap=512):
    b, l, k = a.shape
    d = w.shape[1]
    tm = _row_tile(l, tm_cap)
    return pl.pallas_call(
        _mm_residual_kernel,
        grid=(b, l // tm),
        in_specs=[
            pl.BlockSpec((1, tm, k), lambda i, j: (i, j, 0)),
            pl.BlockSpec((k, d), lambda i, j: (0, 0)),
            pl.BlockSpec((1, tm, d), lambda i, j: (i, j, 0)),
            pl.BlockSpec((1, 1, d), lambda i, j: (i, 0, 0)),
        ],
        out_specs=pl.BlockSpec((1, tm, d), lambda i, j: (i, j, 0)),
        out_shape=jax.ShapeDtypeStruct((b, l, d), F32),
        compiler_params=_cparams(2),
        name="mm_residual",
    )(a, w.astype(BF16), x, gate.reshape(b, 1, d))


def _mla_attn_kernel(q_ref, k_ref, v_ref, o_ref, acc_ref, m_ref, s_ref, *, tk, n_chunks):
    tq = q_ref.shape[1]
    m_ref[...] = jnp.full(m_ref.shape, NEG, F32)
    acc_ref[...] = jnp.zeros(acc_ref.shape, F32)

    def scores(c, slot):
        off = pl.multiple_of(c * tk, tk)
        for hh in range(2):
            q = q_ref[0, :, hh * HEAD_PAD:(hh + 1) * HEAD_PAD]
            kc = k_ref[0, pl.ds(off, tk), hh * HEAD_PAD:(hh + 1) * HEAD_PAD]
            s_ref[slot, hh] = lax.dot_general(q, kc, (((1,), (1,)), ((), ())), preferred_element_type=F32)

    def consume(c, slot):
        off = pl.multiple_of(c * tk, tk)
        vc = v_ref[0, pl.ds(off, tk), :]
        for hh in range(2):
            s = s_ref[slot, hh]
            m_prev = m_ref[hh]
            m_new = jnp.maximum(m_prev, jnp.max(s, axis=-1, keepdims=True))
            p = jnp.exp(s - m_new)
            alpha = jnp.exp(m_prev - m_new)
            acc_ref[hh] = alpha * acc_ref[hh] + jnp.dot(p.astype(BF16), vc, preferred_element_type=F32)
            m_ref[hh] = m_new

    n_pairs = (n_chunks - 1) // 2
    scores(0, 0)

    def body(i, carry):
        c = 2 * i
        scores(c + 1, 1)
        consume(c, 0)
        scores(c + 2, 0)
        consume(c + 1, 1)
        return carry

    lax.fori_loop(0, n_pairs, body, 0)
    if n_chunks - 2 * n_pairs == 1:
        consume(n_chunks - 1, 0)
    else:
        scores(n_chunks - 1, 1)
        consume(n_chunks - 2, 0)
        consume(n_chunks - 1, 1)
    out_a = acc_ref[0, :, :LANES] / acc_ref[0, :, LANES:]
    out_b = acc_ref[1, :, :LANES] / acc_ref[1, :, LANES:]
    lane = lax.broadcasted_iota(jnp.int32, (tq, LANES), 1)
    o_ref[0] = jnp.where(lane < MLA_V, out_a, out_b).astype(o_ref.dtype)


def mla_attention(q, k, v, tq_cap=512):
    b, lq, _ = q.shape
    lk = k.shape[1]
    n_pairs = MLA_HEADS // 2
    tq = _row_tile(lq, tq_cap)
    tk = 768 if lk % 768 == 0 else _row_tile(lk, 512)
    ones = jnp.ones((b, lk, n_pairs, LANES), BF16)
    v_aug = jnp.concatenate([v.astype(BF16).reshape(b, lk, n_pairs, LANES), ones], axis=-1)
    v_aug = v_aug.reshape(b, lk, n_pairs * 2 * LANES)
    kern = functools.partial(_mla_attn_kernel, tk=tk, n_chunks=lk // tk)
    return pl.pallas_call(
        kern,
        grid=(b, n_pairs, lq // tq),
        in_specs=[
            pl.BlockSpec((1, tq, 2 * HEAD_PAD), lambda i, h, j: (i, j, h)),
            pl.BlockSpec((1, lk, 2 * HEAD_PAD), lambda i, h, j: (i, 0, h)),
            pl.BlockSpec((1, lk, 2 * LANES), lambda i, h, j: (i, 0, h)),
        ],
        out_specs=pl.BlockSpec((1, tq, LANES), lambda i, h, j: (i, j, h)),
        out_shape=jax.ShapeDtypeStruct((b, lq, MLA_HEADS * MLA_V), BF16),
        scratch_shapes=[pltpu.VMEM((2, tq, 2 * LANES), F32), pltpu.VMEM((2, tq, 1), F32),
                        pltpu.VMEM((2, 2, tq, tk), F32)],
        compiler_params=_cparams(3),
        name="mla_attn",
    )(q, k, v_aug)


def _lru_scan_kernel(a_ref, b_ref, h0_ref, h_ref, hlast_ref, carry_ref, *, reverse):
    t = pl.program_id(1)
    tt = a_ref.shape[1]

    @pl.when(t == 0)
    def _():
        carry_ref[...] = h0_ref[0]

    a = a_ref[0]
    b = b_ref[0]
    row = lax.broadcasted_iota(jnp.int32, a.shape, 0)
    s = 1
    while s < tt:
        if reverse:
            keep = row < tt - s
            a_s = pltpu.roll(a, tt - s, axis=0)
            b_s = pltpu.roll(b, tt - s, axis=0)
        else:
            keep = row >= s
            a_s = pltpu.roll(a, s, axis=0)
            b_s = pltpu.roll(b, s, axis=0)
        b = jnp.where(keep, a * b_s + b, b)
        a = jnp.where(keep, a * a_s, a)
        s *= 2
    h = a * carry_ref[...] + b
    h_ref[0] = h
    last = h[0:1] if reverse else h[tt - 1:tt]
    carry_ref[...] = last
    hlast_ref[0] = last


def lru_scan(a, b, h0, reverse, tt_cap=256):
    bsz, l, c = a.shape
    tt = _row_tile(l, tt_cap)
    nt = l // tt
    tmap = (lambda i, j: (i, nt - 1 - j, 0)) if reverse else (lambda i, j: (i, j, 0))
    h, hl = pl.pallas_call(
        functools.partial(_lru_scan_kernel, reverse=reverse),
        grid=(bsz, nt),
        in_specs=[
            pl.BlockSpec((1, tt, c), tmap),
            pl.BlockSpec((1, tt, c), tmap),
            pl.BlockSpec((1, 1, c), lambda i, j: (i, 0, 0)),
        ],
        out_specs=[pl.BlockSpec((1, tt, c), tmap), pl.BlockSpec((1, 1, c), lambda i, j: (i, 0, 0))],
        out_shape=[jax.ShapeDtypeStruct((bsz, l, c), F32), jax.ShapeDtypeStruct((bsz, 1, c), F32)],
        scratch_shapes=[pltpu.VMEM((1, c), F32)],
        compiler_params=_cparams(2),
        name="lru_scan",
    )(a, b, h0.reshape(bsz, 1, c))
    return h, hl.reshape(bsz, c)


def _swa_kernel(sink_ref, q_ref, kp_ref, km_ref, kn_ref, vp_ref, vm_ref, vn_ref, kc_ref, vc_ref, o_ref, *, seq):
    g = pl.program_id(1)
    qt = pl.program_id(2)
    tq = q_ref.shape[3]
    rows = SWA_GRP * tq
    q = q_ref[0, 0].reshape(rows, SWA_HD)
    kb = jnp.concatenate([kp_ref[0, 0], km_ref[0, 0], kn_ref[0, 0]], axis=0)
    vb = jnp.concatenate([vp_ref[0, 0], vm_ref[0, 0], vn_ref[0, 0]], axis=0)
    nb = tq + 2 * WINDOW
    nt = (((1,), (1,)), ((), ()))
    s_band = lax.dot_general(q, kb, nt, preferred_element_type=F32)
    s_ctx = lax.dot_general(q, kc_ref[0, 0], nt, preferred_element_type=F32)
    qi = lax.broadcasted_iota(jnp.int32, (rows, nb), 0) & (tq - 1)
    kj = lax.broadcasted_iota(jnp.int32, (rows, nb), 1) - WINDOW
    kpos = qt * tq + kj
    valid = (jnp.abs(kj - qi) <= WINDOW) & (kpos >= 0) & (kpos < seq)
    s_band = jnp.where(valid, s_band, NEG)
    rid = lax.broadcasted_iota(jnp.int32, (rows, 1), 0)
    sink = jnp.full((rows, 1), sink_ref[g * SWA_GRP], F32)
    for j in range(1, SWA_GRP):
        sink = jnp.where(rid >= j * tq, sink_ref[g * SWA_GRP + j], sink)
    m = jnp.maximum(jnp.maximum(jnp.max(s_band, axis=-1, keepdims=True), jnp.max(s_ctx, axis=-1, keepdims=True)), sink)
    p_band = jnp.exp(s_band - m)
    p_ctx = jnp.exp(s_ctx - m)
    denom = jnp.sum(p_band, axis=-1, keepdims=True) + jnp.sum(p_ctx, axis=-1, keepdims=True) + jnp.exp(sink - m)
    o = jnp.dot(p_band.astype(BF16), vb, preferred_element_type=F32)
    o = o + jnp.dot(p_ctx.astype(BF16), vc_ref[0, 0], preferred_element_type=F32)
    o_ref[0, 0] = (o / denom).reshape(SWA_GRP, tq, SWA_HD).astype(o_ref.dtype)


def window_attention(q, k, v, k_ctx, v_ctx, sink, tq=256):
    b, s, hq, hd = q.shape
    hkv = k.shape[2]
    lc = k_ctx.shape[1]
    wb = tq // WINDOW
    n_wblk = s // WINDOW
    qh = jnp.transpose(q.astype(BF16), (0, 2, 1, 3)).reshape(b, hkv, SWA_GRP, s, hd)
    kh, vh, kch, vch = (jnp.transpose(t.astype(BF16), (0, 2, 1, 3)) for t in (k, v, k_ctx, v_ctx))
    prev_map = lambda i, g, j: (i, g, jnp.maximum(j * wb - 1, 0), 0)
    main_map = lambda i, g, j: (i, g, j, 0)
    next_map = lambda i, g, j: (i, g, jnp.minimum((j + 1) * wb, n_wblk - 1), 0)
    ctx_map = lambda i, g, j: (i, g, 0, 0)
    band_specs = [pl.BlockSpec((1, 1, WINDOW, hd), prev_map), pl.BlockSpec((1, 1, tq, hd), main_map),
                  pl.BlockSpec((1, 1, WINDOW, hd), next_map)]
    out = pl.pallas_call(
        functools.partial(_swa_kernel, seq=s),
        grid=(b, hkv, s // tq),
        in_specs=[pl.BlockSpec(memory_space=pltpu.SMEM),
                  pl.BlockSpec((1, 1, SWA_GRP, tq, hd), lambda i, g, j: (i, g, 0, j, 0))]
                 + band_specs + band_specs
                 + [pl.BlockSpec((1, 1, lc, hd), ctx_map), pl.BlockSpec((1, 1, lc, hd), ctx_map)],
        out_specs=pl.BlockSpec((1, 1, SWA_GRP, tq, hd), lambda i, g, j: (i, g, 0, j, 0)),
        out_shape=jax.ShapeDtypeStruct((b, hkv, SWA_GRP, s, hd), BF16),
        compiler_params=_cparams(3),
        name="swa",
    )(sink.astype(F32), qh, kh, kh, kh, vh, vh, vh, kch, vch)
    return jnp.transpose(out.reshape(b, hq, s, hd), (0, 2, 1, 3)).reshape(b, s, hq * hd)


def _split3(x):
    hi = x.astype(BF16)
    r1 = x - hi.astype(F32)
    mid = r1.astype(BF16)
    lo = (r1 - mid.astype(F32)).astype(BF16)
    return hi, mid, lo


def _softplus(x):
    return jnp.maximum(x, 0.0) + jnp.log(1.0 + jnp.exp(-jnp.abs(x)))


def _ssd_kernel(xs_ref, dtc_ref, dtr_ref, bmt_ref, cm_ref, h0_ref, bias_c_ref, bias_r_ref, a_c_ref, a_r_ref,
                dsk_ref, *out_refs, reverse, want_y):
    if want_y:
        y_ref, hfin_ref, st_ref = out_refs
    else:
        hfin_ref, st_ref = out_refs
    c = pl.program_id(1)
    nc = pl.num_programs(1)
    T = SSD_CHUNK

    @pl.when(c == 0)
    def _():
        st_ref[...] = h0_ref[0]

    dt_c = _softplus(dtc_ref[0] + bias_r_ref[...])
    dt_r = _softplus(dtr_ref[0] + bias_c_ref[...])
    a_c = dt_c * a_r_ref[...]
    a_r = dt_r * a_c_ref[...]
    ii = lax.broadcasted_iota(jnp.int32, (T, T), 0)
    jj = lax.broadcasted_iota(jnp.int32, (T, T), 1)
    causal = (jj >= ii) if reverse else (jj <= ii)
    tri = jnp.where(causal, 1.0, 0.0).astype(BF16)
    tri_t = jnp.where((ii >= jj) if reverse else (ii <= jj), 1.0, 0.0).astype(BF16)
    cum_c = sum(jnp.dot(tri, p, preferred_element_type=F32) for p in _split3(a_c))
    cum_r = sum(jnp.dot(p, tri_t, preferred_element_type=F32) for p in _split3(a_r))
    edge = 0 if reverse else T - 1
    tot_c = cum_c[edge:edge + 1, :]
    cm = cm_ref[0]
    xs = xs_ref[0]
    ys = []
    for g in range(SSD_GROUPS):
        cm_g = cm[:, g * SSD_STATE:(g + 1) * SSD_STATE]
        bmt_g = bmt_ref[0, g]
        cb = jnp.dot(cm_g, bmt_g, preferred_element_type=F32) if want_y else None
        for hh in range(SSD_HEADS // SSD_GROUPS):
            h = g * (SSD_HEADS // SSD_GROUPS) + hh
            col = cum_c[:, h:h + 1]
            rw = cum_r[h:h + 1, :]
            x_h = xs[:, h * SSD_HD:(h + 1) * SSD_HD]
            xdt = (x_h * dt_c[:, h:h + 1]).astype(BF16)
            st = st_ref[h]
            if want_y:
                decay = jnp.where(causal, jnp.exp(jnp.minimum(col - rw, 0.0)), 0.0)
                y_diag = jnp.dot((cb * decay).astype(BF16), xdt, preferred_element_type=F32)
                y_off = jnp.dot(cm_g, st.astype(BF16), preferred_element_type=F32) * jnp.exp(col)
                ys.append(y_diag + y_off + dsk_ref[:, h * SSD_HD:(h + 1) * SSD_HD] * x_h)
            to_edge = jnp.exp(tot_c[:, h:h + 1] - rw)
            upd = jnp.dot((bmt_g.astype(F32) * to_edge).astype(BF16), xdt, preferred_element_type=F32)
            st_ref[h] = jnp.exp(tot_c[:, h:h + 1]) * st + upd
    if want_y:
        y_ref[0] = jnp.concatenate(ys, axis=-1)

    @pl.when(c == nc - 1)
    def _():
        hfin_ref[0] = st_ref[...]


def ssd_scan(xs, dt_raw, bm, cm, a_log, dt_bias, d_skip, h0, reverse, want_y):
    b, l, _ = xs.shape
    nc = l // SSD_CHUNK
    hds, n, p = SSD_HEADS, SSD_STATE, SSD_HD
    a_neg = -jnp.exp(a_log.astype(F32))
    bmt = jnp.transpose(bm.astype(BF16).reshape(b, l, SSD_GROUPS, n), (0, 2, 3, 1))
    dtr = jnp.transpose(dt_raw, (0, 2, 1))
    cmap = (lambda i, j: (i, nc - 1 - j, 0)) if reverse else (lambda i, j: (i, j, 0))
    cmap_t = (lambda i, j: (i, 0, nc - 1 - j)) if reverse else (lambda i, j: (i, 0, j))
    cmap_bt = (lambda i, j: (i, 0, 0, nc - 1 - j)) if reverse else (lambda i, j: (i, 0, 0, j))
    full2 = lambda i, j: (0, 0)
    st_map = lambda i, j: (i, 0, 0, 0)
    y_spec = [pl.BlockSpec((1, SSD_CHUNK, hds * p), cmap)] if want_y else []
    y_shape = [jax.ShapeDtypeStruct((b, l, hds * p), F32)] if want_y else []
    outs = pl.pallas_call(
        functools.partial(_ssd_kernel, reverse=reverse, want_y=want_y),
        grid=(b, nc),
        in_specs=[
            pl.BlockSpec((1, SSD_CHUNK, hds * p), cmap),
            pl.BlockSpec((1, SSD_CHUNK, hds), cmap),
            pl.BlockSpec((1, hds, SSD_CHUNK), cmap_t),
            pl.BlockSpec((1, SSD_GROUPS, n, SSD_CHUNK), cmap_bt),
            pl.BlockSpec((1, SSD_CHUNK, SSD_GROUPS * n), cmap),
            pl.BlockSpec((1, hds, n, p), st_map),
            pl.BlockSpec((hds, 1), full2),
            pl.BlockSpec((1, hds), full2),
            pl.BlockSpec((hds, 1), full2),
            pl.BlockSpec((1, hds), full2),
            pl.BlockSpec((1, hds * p), full2),
        ],
        out_specs=y_spec + [pl.BlockSpec((1, hds, n, p), st_map)],
        out_shape=y_shape + [jax.ShapeDtypeStruct((b, hds, n, p), F32)],
        scratch_shapes=[pltpu.VMEM((hds, n, p), F32)],
        compiler_params=_cparams(2),
        name="ssd_scan",
    )(xs, dt_raw, dtr, bmt, cm.astype(BF16), h0, dt_bias.astype(F32).reshape(hds, 1),
      dt_bias.astype(F32).reshape(1, hds), a_neg.reshape(hds, 1), a_neg.reshape(1, hds),
      jnp.repeat(d_skip.astype(F32), p).reshape(1, hds * p))
    return (outs[0], outs[1]) if want_y else (None, outs[0])


def _router_kernel(x_ref, g_ref, sh_ref, sc_ref, wt_ref, rb_ref, cin_ref, h_ref, sel_ref, cnt_ref):
    first = (pl.program_id(0) == 0) & (pl.program_id(1) == 0)

    @pl.when(first)
    def _():
        cnt_ref[...] = cin_ref[...]

    x = x_ref[0]
    tm = x.shape[0]
    y = x * lax.rsqrt(jnp.mean(x * x, axis=-1, keepdims=True) + EPS)
    h = (y * g_ref[...]) * (1.0 + sc_ref[0]) + sh_ref[0]
    h_ref[0] = h.astype(BF16)
    hh = h.astype(BF16)
    hl = (h - hh.astype(F32)).astype(BF16)
    w = wt_ref[...]
    wh = w.astype(BF16)
    wl = (w - wh.astype(F32)).astype(BF16)
    nt = (((1,), (1,)), ((), ()))
    logits = lax.dot_general(wh, hh, nt, preferred_element_type=F32)
    logits = logits + lax.dot_general(wh, hl, nt, preferred_element_type=F32)
    logits = logits + lax.dot_general(wl, hh, nt, preferred_element_type=F32)
    scores = 1.0 / (1.0 + jnp.exp(-logits))
    biased = scores + rb_ref[...]
    io = lax.broadcasted_iota(jnp.int32, (EXPERTS_PER_GROUP, tm), 0)
    gs, i1s, i2s = [], [], []
    for g in range(N_GROUPS):
        bg = biased[g * EXPERTS_PER_GROUP:(g + 1) * EXPERTS_PER_GROUP]
        m1 = jnp.max(bg, axis=0, keepdims=True)
        i1 = jnp.min(jnp.where(bg == m1, io, EXPERTS_PER_GROUP), axis=0, keepdims=True)
        bg2 = jnp.where(io == i1, -jnp.inf, bg)
        m2 = jnp.max(bg2, axis=0, keepdims=True)
        i2 = jnp.min(jnp.where(bg2 == m2, io, EXPERTS_PER_GROUP), axis=0, keepdims=True)
        gs.append(m1 + m2)
        i1s.append(i1)
        i2s.append(i2)
    gmax = jnp.maximum(jnp.maximum(gs[0], gs[1]), jnp.maximum(gs[2], gs[3]))
    gsel = jnp.full((1, tm), N_GROUPS - 1, jnp.int32)
    l1, l2 = i1s[N_GROUPS - 1], i2s[N_GROUPS - 1]
    for g in range(N_GROUPS - 2, -1, -1):
        hit = gs[g] == gmax
        gsel = jnp.where(hit, g, gsel)
        l1 = jnp.where(hit, i1s[g], l1)
        l2 = jnp.where(hit, i2s[g], l2)
    e1 = gsel * EXPERTS_PER_GROUP + l1
    e2 = gsel * EXPERTS_PER_GROUP + l2
    eid = lax.broadcasted_iota(jnp.int32, (N_EXPERTS, tm), 0)
    oh1 = eid == e1
    oh2 = eid == e2
    s1 = jnp.sum(jnp.where(oh1, scores, 0.0), axis=0, keepdims=True)
    s2 = jnp.sum(jnp.where(oh2, scores, 0.0), axis=0, keepdims=True)
    picked = jnp.where(oh1, 1.0, jnp.where(oh2, 1.0, 0.0))
    before = (lax.broadcasted_iota(jnp.int32, (tm, tm), 0) < lax.broadcasted_iota(jnp.int32, (tm, tm), 1))
    prior = jnp.dot(picked.astype(BF16), jnp.where(before, 1.0, 0.0).astype(BF16), preferred_element_type=F32)
    prior = prior + cnt_ref[:, 0:1]
    r1 = jnp.sum(jnp.where(oh1, prior, 0.0), axis=0, keepdims=True)
    r2 = jnp.sum(jnp.where(oh2, prior, 0.0), axis=0, keepdims=True)
    cnt_ref[...] = cnt_ref[...] + jnp.sum(picked, axis=1, keepdims=True)
    zero = jnp.zeros((1, tm), F32)
    sel_ref[...] = jnp.concatenate([e1.astype(F32), e2.astype(F32), s1 / (s1 + s2), s2 / (s1 + s2), r1, r2, zero, zero],
                                   axis=0)


def route_tokens(x, g, shift, scale, router_w, router_b, counts_in, tm_cap=512):
    b, l, d = x.shape
    tm = _row_tile(l, tm_cap)
    nt = l // tm
    return pl.pallas_call(
        _router_kernel,
        grid=(b, nt),
        in_specs=[
            pl.BlockSpec((1, tm, d), lambda i, j: (i, j, 0)),
            pl.BlockSpec((1, d), lambda i, j: (0, 0)),
            pl.BlockSpec((1, 1, d), lambda i, j: (i, 0, 0)),
            pl.BlockSpec((1, 1, d), lambda i, j: (i, 0, 0)),
            pl.BlockSpec((N_EXPERTS, d), lambda i, j: (0, 0)),
            pl.BlockSpec((N_EXPERTS, 1), lambda i, j: (0, 0)),
            pl.BlockSpec((N_EXPERTS, LANES), lambda i, j: (0, 0)),
        ],
        out_specs=[
            pl.BlockSpec((1, tm, d), lambda i, j: (i, j, 0)),
            pl.BlockSpec((8, tm), lambda i, j: (0, i * nt + j)),
            pl.BlockSpec((N_EXPERTS, LANES), lambda i, j: (0, 0)),
        ],
        out_shape=[jax.ShapeDtypeStruct((b, l, d), BF16), jax.ShapeDtypeStruct((8, b * l), F32),
                   jax.ShapeDtypeStruct((N_EXPERTS, LANES), F32)],
        compiler_params=_cparams(2),
        name="router",
    )(x, g.reshape(1, d), shift.reshape(b, 1, d), scale.reshape(b, 1, d), router_w.T,
      router_b.astype(F32).reshape(N_EXPERTS, 1), counts_in)


def _expert_kernel(be_ref, nu_ref, x_ref, wg_ref, wu_ref, wd_ref, o_ref, wg_s, wu_s, wd_s):
    i = pl.program_id(0)
    live = i < nu_ref[0]
    fresh = (i == 0) | (be_ref[i] != be_ref[jnp.maximum(i - 1, 0)])

    @pl.when(live & fresh)
    def _():
        wg_s[...] = wg_ref[0].astype(BF16)
        wu_s[...] = wu_ref[0].astype(BF16)
        wd_s[...] = wd_ref[0].astype(BF16)

    @pl.when(live)
    def _():
        x = x_ref[...]
        gate = jnp.dot(x, wg_s[...], preferred_element_type=F32)
        up = jnp.dot(x, wu_s[...], preferred_element_type=F32)
        act = (gate * (1.0 / (1.0 + jnp.exp(-gate))) * up).astype(BF16)
        o_ref[...] = jnp.dot(act, wd_s[...], preferred_element_type=F32)

    @pl.when(jnp.logical_not(live))
    def _():
        o_ref[...] = jnp.zeros(o_ref.shape, F32)


def expert_blocks(xb, block_expert, n_used, w_gate, w_up, w_down):
    rows, d = xb.shape
    n_blocks = rows // MOE_BLK
    de = w_gate.shape[-1]
    grid_spec = pltpu.PrefetchScalarGridSpec(
        num_scalar_prefetch=2,
        grid=(n_blocks,),
        in_specs=[
            pl.BlockSpec((MOE_BLK, d), lambda i, be, nu: (i, 0)),
            pl.BlockSpec((1, d, de), lambda i, be, nu: (be[i], 0, 0)),
            pl.BlockSpec((1, d, de), lambda i, be, nu: (be[i], 0, 0)),
            pl.BlockSpec((1, de, d), lambda i, be, nu: (be[i], 0, 0)),
        ],
        out_specs=pl.BlockSpec((MOE_BLK, d), lambda i, be, nu: (i, 0)),
        scratch_shapes=[pltpu.VMEM((d, de), BF16), pltpu.VMEM((d, de), BF16), pltpu.VMEM((de, d), BF16)],
    )
    return pl.pallas_call(
        _expert_kernel,
        grid_spec=grid_spec,
        out_shape=jax.ShapeDtypeStruct((rows, d), F32),
        compiler_params=_cparams(1),
        name="experts",
    )(block_expert, n_used, xb, w_gate, w_up, w_down)


def moe_dispatch(h_parts, sel, counts, w_gate, w_up, w_down):
    h = jnp.concatenate(h_parts, axis=0) if len(h_parts) > 1 else h_parts[0]
    n, d = h.shape
    m = n * TOP_K
    n_blocks = -(-m // MOE_BLK) + N_EXPERTS
    e_sel = sel[0:2].astype(jnp.int32)
    w_sel = sel[2:4]
    rank = sel[4:6].astype(jnp.int32)
    cnt = counts[:, 0].astype(jnp.int32)
    padded = (cnt + MOE_BLK - 1) // MOE_BLK * MOE_BLK
    pad_end = jnp.cumsum(padded)
    pad_start = pad_end - padded
    dest = pad_start[e_sel] + rank
    tok = jnp.broadcast_to(jnp.arange(n, dtype=jnp.int32)[None, :], (TOP_K, n))
    slot_tok = jnp.full((n_blocks * MOE_BLK,), n, jnp.int32).at[dest.reshape(-1)].set(tok.reshape(-1))
    block_expert = jnp.minimum(jnp.searchsorted(pad_end, jnp.arange(n_blocks) * MOE_BLK, side='right'),
                               N_EXPERTS - 1).astype(jnp.int32)
    n_used = (pad_end[-1] // MOE_BLK).astype(jnp.int32).reshape(1)
    h_pad = jnp.concatenate([h, jnp.zeros((1, d), BF16)], axis=0)
    xb = h_pad[slot_tok]
    yb = expert_blocks(xb, block_expert, n_used, w_gate, w_up, w_down)
    return yb[dest[0]] * w_sel[0][:, None] + yb[dest[1]] * w_sel[1][:, None]


def rmsnorm(x, g):
    xf = x.astype(F32)
    y = xf * lax.rsqrt(jnp.mean(xf * xf, axis=-1, keepdims=True) + EPS)
    return y * g.astype(F32)


def grid_positions(n):
    rows = n // GRID_W
    row = jnp.repeat(jnp.arange(rows, dtype=F32), GRID_W)
    col = jnp.tile(jnp.arange(GRID_W, dtype=F32), rows)
    return row, col


def rope_1d(x, pos):
    f = x.shape[-1] // 2
    inv = ROPE_BASE ** (-jnp.arange(f, dtype=F32) / f)
    ang = pos[:, None] * inv[None, :]
    cos = jnp.cos(ang)[None, :, None, :]
    sin = jnp.sin(ang)[None, :, None, :]
    x1 = x[..., :f]
    x2 = x[..., f:]
    return jnp.concatenate([x1 * cos - x2 * sin, x1 * sin + x2 * cos], axis=-1)


def rope_2d(x, row, col):
    half = x.shape[-1] // 2
    return jnp.concatenate([rope_1d(x[..., :half], row), rope_1d(x[..., half:], col)], axis=-1)


def centred_dwconv(x, w, b):
    y = lax.conv_general_dilated(x, w[:, None, :].astype(x.dtype), window_strides=(1,), padding=[CONV_PAD],
                                 dimension_numbers=('NWC', 'WIO', 'NWC'), feature_group_count=x.shape[-1])
    return y + b.astype(x.dtype)


def block_diag_dense(w):
    nb, di, do = w.shape
    eye = jnp.eye(nb, dtype=w.dtype)
    return jnp.einsum('hij,hk->hikj', w, eye).reshape(nb * di, nb * do)


def mixer_ab(x_lat, x_ctx, g1, mod_l, mod_c, row, col, need_ctx, w_in, w_out, q_norm_g, w_q_up, kv_norm_g, w_kv_up,
             q_head_g, k_head_g, conv_w, conv_b, w_a, b_a, w_x, b_x, lam):
    bsz, n_lat, _ = x_lat.shape
    n_ctx = x_ctx.shape[1]
    cuts = np.cumsum([MLA_Q_RANK, MLA_KV_RANK, MLA_ROPE, LRU_WIDTH]).tolist()
    p_l = norm_mod_matmul(x_lat, g1, mod_l[:, 0], mod_l[:, 1], w_in)
    p_c = norm_mod_matmul(x_ctx, g1, mod_c[:, 0], mod_c[:, 1], w_in)
    cq_l, ckv_l, kr_l, xr_l, gr_l = jnp.split(p_l, cuts, axis=-1)
    cq_c, ckv_c, kr_c, xr_c, gr_c = jnp.split(p_c, cuts, axis=-1)

    def up(t, g, w):
        b, n, r = t.shape
        return matmul(rmsnorm(t, g).reshape(b * n, r).astype(BF16), w).reshape(b, n, -1)

    def heads_q(cq):
        b, n = cq.shape[:2]
        return rmsnorm(up(cq, q_norm_g, w_q_up).reshape(b, n, MLA_HEADS, MLA_QK), q_head_g)

    def heads_kv(ckv, kr):
        b, n = ckv.shape[:2]
        kv = up(ckv, kv_norm_g, w_kv_up).reshape(b, n, MLA_HEADS, MLA_NOPE + MLA_V)
        k_rope = jnp.broadcast_to(kr[:, :, None, :], (b, n, MLA_HEADS, MLA_ROPE))
        k = rmsnorm(jnp.concatenate([kv[..., :MLA_NOPE], k_rope], axis=-1), k_head_g)
        return k, kv[..., MLA_NOPE:]

    def rope_tail(t):
        return jnp.concatenate([t[..., :MLA_NOPE], rope_2d(t[..., MLA_NOPE:], row, col)], axis=-1)

    def pad_heads(t, scale=1.0):
        b, n = t.shape[:2]
        t = jnp.pad(t * scale, ((0, 0), (0, 0), (0, 0), (0, HEAD_PAD - MLA_QK)))
        return t.astype(BF16).reshape(b, n, MLA_HEADS * HEAD_PAD)

    k_c, v_c = heads_kv(ckv_c, kr_c)
    k_l, v_l = heads_kv(ckv_l, kr_l)
    q_l = rope_tail(heads_q(cq_l))
    k_l = rope_tail(k_l)
    k_all = jnp.concatenate([pad_heads(k_c), pad_heads(k_l)], axis=1)
    v_all = jnp.concatenate([v_c, v_l], axis=1).reshape(bsz, n_ctx + n_lat, MLA_HEADS * MLA_V)
    attn_l = mla_attention(pad_heads(q_l, MLA_QK ** -0.5), k_all, v_all)

    xc_l = centred_dwconv(xr_l, conv_w, conv_b)
    xc_c = centred_dwconv(xr_c, conv_w, conv_b)
    w_gates = jnp.concatenate([block_diag_dense(w_a[0]), block_diag_dense(w_x[0]),
                               block_diag_dense(w_a[1]), block_diag_dense(w_x[1])], axis=1)

    def gates(xc):
        b, n, c = xc.shape
        return matmul(xc.reshape(b * n, c), w_gates).reshape(b, n, 4, c)

    gt_l, gt_c = gates(xc_l), gates(xc_c)

    def scan_inputs(xc, gt, d):
        r = jax.nn.sigmoid(gt[:, :, 2 * d] + b_a[d])
        i = jax.nn.sigmoid(gt[:, :, 2 * d + 1] + b_x[d])
        log_a = -LRU_C * r * jax.nn.softplus(-lam[d].astype(F32))
        return jnp.exp(log_a), jnp.sqrt(-jnp.expm1(2.0 * log_a)) * (i * xc)

    zero = jnp.zeros((bsz, LRU_WIDTH), F32)
    hs_l, hs_c = [], []
    for d, rev in enumerate((False, True)):
        h_c, h_end = lru_scan(*scan_inputs(xc_c, gt_c, d), zero, rev)
        h_l, _ = lru_scan(*scan_inputs(xc_l, gt_l, d), h_end, rev)
        hs_l.append(h_l)
        hs_c.append(h_c)
    lru_l = (hs_l[0] + hs_l[1]) * jax.nn.gelu(gr_l)
    y_in_l = jnp.concatenate([attn_l, lru_l.astype(BF16)], axis=-1)
    x_lat = matmul_residual(y_in_l, w_out, x_lat, mod_l[:, 2])
    if not need_ctx:
        return x_lat, None
    attn_c = mla_attention(pad_heads(heads_q(cq_c), MLA_QK ** -0.5), pad_heads(k_c),
                           v_c.reshape(bsz, n_ctx, MLA_HEADS * MLA_V))
    lru_c = (hs_c[0] + hs_c[1]) * jax.nn.gelu(gr_c)
    y_in_c = jnp.concatenate([attn_c, lru_c.astype(BF16)], axis=-1)
    x_ctx = matmul_residual(y_in_c, w_out, x_ctx, mod_c[:, 2])
    return x_lat, x_ctx


def mixer_cd(x_lat, x_ctx, g1, mod_l, mod_c, row, col, need_ctx, w_in, w_out, q_head_g, k_head_g, sink,
             conv_w, conv_b, a_log, dt_bias, d_skip, norm_g):
    q_w = SWA_HEADS * SWA_HD
    kv_w = SWA_KV_HEADS * SWA_HD
    cuts = np.cumsum([q_w, kv_w, kv_w, SSD_INNER, SSD_CONV_DIM]).tolist()
    p_l = norm_mod_matmul(x_lat, g1, mod_l[:, 0], mod_l[:, 1], w_in)
    p_c = norm_mod_matmul(x_ctx, g1, mod_c[:, 0], mod_c[:, 1], w_in)
    q_l, k_l, v_l, z_l, xbc_l, dt_l = jnp.split(p_l, cuts, axis=-1)
    q_c, k_c, v_c, z_c, xbc_c, dt_c = jnp.split(p_c, cuts, axis=-1)
    bsz, n_lat = x_lat.shape[:2]
    n_ctx = x_ctx.shape[1]

    def heads(t, nh):
        return t.reshape(t.shape[0], t.shape[1], nh, SWA_HD)

    k_ch = rmsnorm(heads(k_c, SWA_KV_HEADS), k_head_g)
    v_ch = heads(v_c, SWA_KV_HEADS)
    q_lh = rope_2d(rmsnorm(heads(q_l, SWA_HEADS), q_head_g), row, col) * SWA_HD ** -0.5
    k_lh = rope_2d(rmsnorm(heads(k_l, SWA_KV_HEADS), k_head_g), row, col)
    attn_l = window_attention(q_lh, k_lh, heads(v_l, SWA_KV_HEADS), k_ch, v_ch, sink)

    def ssd_prep(xbc):
        u = jax.nn.silu(centred_dwconv(xbc, conv_w, conv_b))
        return (u[..., :SSD_INNER], u[..., SSD_INNER:SSD_INNER + SSD_GROUPS * SSD_STATE],
                u[..., SSD_INNER + SSD_GROUPS * SSD_STATE:])

    xs_l, bm_l, cm_l = ssd_prep(xbc_l)
    xs_c, bm_c, cm_c = ssd_prep(xbc_c)
    h0 = jnp.zeros((bsz, SSD_HEADS, SSD_STATE, SSD_HD), F32)
    ys_l, ys_c = [], []
    for d in range(2):
        sl = slice(d * SSD_HEADS, (d + 1) * SSD_HEADS)
        y_c, h_c = ssd_scan(xs_c, dt_c[..., sl], bm_c, cm_c, a_log[d], dt_bias[d], d_skip[d], h0, d == 1, need_ctx)
        y_l, _ = ssd_scan(xs_l, dt_l[..., sl], bm_l, cm_l, a_log[d], dt_bias[d], d_skip[d], h_c, d == 1, True)
        ys_l.append(y_l)
        ys_c.append(y_c)

    def gated_out(ys, z):
        return rmsnorm((ys[0] + ys[1]) * jax.nn.silu(z), norm_g)

    y_in_l = jnp.concatenate([attn_l, gated_out(ys_l, z_l).astype(BF16)], axis=-1)
    x_lat = matmul_residual(y_in_l, w_out, x_lat, mod_l[:, 2])
    if not need_ctx:
        return x_lat, None
    raise NotImplementedError("context output of the windowed/SSD mixer is only needed for deeper stacks")


def kernel(x, c, ctx, c_ctx, ada_w, ada_b, norm1_g, norm2_g, ab_w_in, ab_w_out, mla_q_norm_g, mla_w_q_up,
           mla_kv_norm_g, mla_w_kv_up, mla_q_head_g, mla_k_head_g, lru_conv_w, lru_conv_b, lru_w_a, lru_b_a,
           lru_w_x, lru_b_x, lru_lambda, cd_w_in, cd_w_out, swa_q_head_g, swa_k_head_g, swa_sink, ssd_conv_w,
           ssd_conv_b, ssd_a_log, ssd_dt_bias, ssd_d, ssd_norm_g, router_w, router_b, moe_w_gate, moe_w_up,
           moe_w_down):
    bsz, n_lat, d = x.shape
    n_ctx = ctx.shape[1]
    row, col = grid_positions(n_lat)
    x_lat, x_ctx = x, ctx
    cc = jnp.concatenate([c, c_ctx[None, :], jnp.zeros((8 - (bsz + 1) % 8, d), F32)], axis=0)
    for layer in range(DEPTH):
        need_ctx = layer < DEPTH - 1
        mod = (matmul_precise(jax.nn.silu(cc), ada_w[layer])[:bsz + 1] + ada_b[layer]).reshape(bsz + 1, N_MOD, d)
        mod_l = mod[:bsz]
        mod_c = jnp.broadcast_to(mod[bsz:], (bsz, N_MOD, d))
        i = layer // 2
        if layer % 2 == 0:
            x_lat, x_ctx = mixer_ab(x_lat, x_ctx, norm1_g[layer], mod_l, mod_c, row, col, need_ctx, ab_w_in[i],
                                    ab_w_out[i], mla_q_norm_g[i], mla_w_q_up[i], mla_kv_norm_g[i], mla_w_kv_up[i],
                                    mla_q_head_g[i], mla_k_head_g[i], lru_conv_w[i], lru_conv_b[i],
                                    lru_w_a[i], lru_b_a[i], lru_w_x[i], lru_b_x[i], lru_lambda[i])
        else:
            x_lat, x_ctx = mixer_cd(x_lat, x_ctx, norm1_g[layer], mod_l, mod_c, row, col, need_ctx, cd_w_in[i],
                                    cd_w_out[i], swa_q_head_g[i], swa_k_head_g[i], swa_sink[i], ssd_conv_w[i],
                                    ssd_conv_b[i], ssd_a_log[i], ssd_dt_bias[i], ssd_d[i], ssd_norm_g[i])

        counts0 = jnp.zeros((N_EXPERTS, LANES), F32)
        h2_lat, sel_l, counts = route_tokens(x_lat, norm2_g[layer], mod_l[:, 3], mod_l[:, 4], router_w, router_b,
                                             counts0)
        h_parts, sels = [h2_lat.reshape(bsz * n_lat, d)], [sel_l]
        if need_ctx:
            h2_ctx, sel_c, counts = route_tokens(x_ctx, norm2_g[layer], mod_c[:, 3], mod_c[:, 4], router_w,
                                                 router_b, counts)
            h_parts.append(h2_ctx.reshape(bsz * n_ctx, d))
            sels.append(sel_c)
        f = moe_dispatch(h_parts, jnp.concatenate(sels, axis=1) if need_ctx else sel_l, counts,
                         moe_w_gate[layer], moe_w_up[layer], moe_w_down[layer])
        if need_ctx:
            x_ctx = x_ctx + mod_c[:, 5][:, None, :] * f[bsz * n_lat:].reshape(bsz, n_ctx, d)
        x_lat = x_lat + mod_l[:, 5][:, None, :] * f[:bsz * n_lat].reshape(bsz, n_lat, d)
    return x_lat
```
